```python
import numpy as np
import jax
import jax.numpy as jnp
from jax import lax

D_MODEL = 1024
BATCH = 8
SEQ = 4096
DEPTH = 1
DEC_BATCH = 32
DEC_SEQ = 1
PAST_LEN = 16384
PAGE_SIZE = 128

A_GROUPS = ((128, 1), (512, 4), (2048, 16))
A_N_GROUPS = 3
A_HEADS_PER_GROUP = 4
A_HEAD_DIM = 64
A_HEADS = A_N_GROUPS * A_HEADS_PER_GROUP
A_WIDTH = A_HEADS * A_HEAD_DIM
A_OUT = A_HEADS_PER_GROUP * A_HEAD_DIM
ROPE_THETA = 10000.0
Q_BLOCK = 128
GLA_HEADS = 4
GLA_HEAD_DK = 128
GLA_HEAD_DV = 256
GLA_DK = GLA_HEADS * GLA_HEAD_DK
GLA_DV = GLA_HEADS * GLA_HEAD_DV
GLA_GATE_RANK = 16
GLA_GATE_NORM = 16.0
GLA_CHUNK = 64
D_FF = 4 * D_MODEL
IN_WIDTH = 3 * A_WIDTH + 2 * GLA_DK + 2 * GLA_DV + GLA_GATE_RANK + 2 * D_MODEL
EPS = 1e-6

kernel_name = 'dilated_gla_hybrid_step'


def rmsnorm(x, g):
    xf = x.astype(jnp.float32)
    y = xf * lax.rsqrt(jnp.mean(xf * xf, axis=-1, keepdims=True) + EPS)
    return (y * g.astype(jnp.float32)).astype(x.dtype)


def rope(x, pos):
    half = A_HEAD_DIM // 2
    inv = ROPE_THETA ** (-jnp.arange(half, dtype=jnp.float32) / half)
    ang = pos.astype(jnp.float32)[:, None] * inv[None, :]
    cos = jnp.cos(ang)[None, :, None, :]
    sin = jnp.sin(ang)[None, :, None, :]
    xf = x.astype(jnp.float32)
    x1, x2 = xf[..., :half], xf[..., half:]
    return jnp.concatenate([x1 * cos - x2 * sin, x2 * cos + x1 * sin], axis=-1).astype(x.dtype)


def project(u, w_in, w_gk2, b_gk, pos):
    b, t, _ = u.shape
    sizes = (A_WIDTH, A_WIDTH, A_WIDTH, GLA_DK, GLA_DK, GLA_DV, GLA_DV, GLA_GATE_RANK, D_MODEL, D_MODEL)
    offs = [int(o) for o in np.cumsum(sizes)[:-1]]
    qa, ka, va, qb, kb, vb, rb, glr, ga, gb = jnp.split(u @ w_in, offs, axis=-1)
    qa = rope(qa.reshape(b, t, A_HEADS, A_HEAD_DIM), pos)
    ka = rope(ka.reshape(b, t, A_HEADS, A_HEAD_DIM), pos)
    va = va.reshape(b, t, A_HEADS, A_HEAD_DIM)
    f32 = jnp.float32
    qb = qb.reshape(b, t, GLA_HEADS, GLA_HEAD_DK).astype(f32) * (GLA_HEAD_DK ** -0.5)
    kb = kb.reshape(b, t, GLA_HEADS, GLA_HEAD_DK).astype(f32)
    vb = vb.reshape(b, t, GLA_HEADS, GLA_HEAD_DV).astype(f32)
    gk = jax.nn.log_sigmoid((glr @ w_gk2 + b_gk).astype(f32)) / GLA_GATE_NORM
    gk = gk.reshape(b, t, GLA_HEADS, GLA_HEAD_DK)
    return qa, ka, va, qb, kb, vb, gk, rb, ga, gb


def gathered_attn(q, k, v, idx, valid):
    safe = jnp.maximum(idx, 0)
    kk = jnp.take(k, safe, axis=1).astype(jnp.float32)
    vv = jnp.take(v, safe, axis=1).astype(jnp.float32)
    s = jnp.einsum('bqhd,bqkhd->bqhk', q.astype(jnp.float32), kk) * (A_HEAD_DIM ** -0.5)
    s = jnp.where(valid[None, :, None, :], s, -jnp.inf)
    m = jnp.max(s, axis=-1)
    p = jnp.exp(s - m[..., None])
    l = jnp.sum(p, axis=-1)
    num = jnp.einsum('bqhk,bqkhd->bqhd', p, vv)
    return m, l, num


def merge_dilation_groups(stats):
    m = jnp.stack([st[0] for st in stats])
    l = jnp.stack([st[1] for st in stats])
    num = jnp.stack([st[2] for st in stats])
    w = jnp.exp(m - jnp.max(m, axis=0, keepdims=True))
    o = jnp.sum(w[..., None] * num, axis=0) / jnp.sum(w * l, axis=0)[..., None]
    b, q = o.shape[:2]
    return o.reshape(b, q, A_OUT)


def group_slice(g):
    return slice(g * A_HEADS_PER_GROUP, (g + 1) * A_HEADS_PER_GROUP)


def dilated_prompt(qa, ka, va):
    b, s = qa.shape[:2]
    kg = [ka[:, :, group_slice(g)] for g in range(A_N_GROUPS)]
    vg = [va[:, :, group_slice(g)] for g in range(A_N_GROUPS)]

    def block(bi):
        t0 = bi * Q_BLOCK
        t = t0 + jnp.arange(Q_BLOCK)
        qblk = lax.dynamic_slice_in_dim(qa, t0, Q_BLOCK, axis=1)
        stats = []
        for g, (window, dil) in enumerate(A_GROUPS):
            idx = t[:, None] - dil * jnp.arange(window // dil + 1)[None, :]
            stats.append(gathered_attn(qblk[:, :, group_slice(g)], kg[g], vg[g], idx, idx >= 0))
        return merge_dilation_groups(stats)

    out = lax.map(block, jnp.arange(s // Q_BLOCK))
    return out.transpose(1, 0, 2, 3).reshape(b, s, A_OUT).astype(qa.dtype)


def prompt_buffers(ka, va):
    s = ka.shape[1]
    out = []
    for g, (window, _) in enumerate(A_GROUPS):
        keep = min(window, s)
        out += [ka[:, s - keep:, group_slice(g)], va[:, s - keep:, group_slice(g)]]
    return out


def dilated_sample(qa, ka, va, bufs):
    t = qa.shape[1]
    i = jnp.arange(t)
    stats, new = [], []
    for g, (window, dil) in enumerate(A_GROUPS):
        hs = group_slice(g)
        k_buf, v_buf = bufs[2 * g], bufs[2 * g + 1]
        kc = jnp.concatenate([k_buf, ka[:, :, hs].astype(k_buf.dtype)], axis=1)
        vc = jnp.concatenate([v_buf, va[:, :, hs].astype(v_buf.dtype)], axis=1)
        idx = k_buf.shape[1] + i[:, None] - dil * jnp.arange(window // dil + 1)[None, :]
        stats.append(gathered_attn(qa[:, :, hs], kc, vc, idx, idx >= 0))
        total = kc.shape[1]
        keep = min(window, total)
        new += [kc[:, total - keep:], vc[:, total - keep:]]
    return merge_dilation_groups(stats).astype(qa.dtype), new


def gla_chunked(q, k, v, g, s0):
    b, t, h, _ = q.shape
    dv = v.shape[-1]
    n = t // GLA_CHUNK

    def chunks(z):
        return z.reshape(b, n, GLA_CHUNK, h, z.shape[-1]).transpose(1, 0, 3, 2, 4)

    causal = jnp.tril(jnp.ones((GLA_CHUNK, GLA_CHUNK), dtype=bool))

    def step(state, inp):
        qc, kc, vc, gc = inp
        cum = jnp.cumsum(gc, axis=2)
        diff = cum[:, :, :, None, :] - cum[:, :, None, :, :]
        decay = jnp.exp(jnp.where(causal[None, None, :, :, None], diff, -jnp.inf))
        scores = jnp.einsum('bhtsd,bhsd->bhts', qc[:, :, :, None, :] * decay, kc)
        o = (jnp.einsum('bhts,bhsv->bhtv', scores, vc)
             + jnp.einsum('bhtd,bhdv->bhtv', qc * jnp.exp(cum), state))
        last = cum[:, :, -1:, :]
        state = (jnp.exp(last[:, :, 0, :])[..., None] * state
                 + jnp.einsum('bhsd,bhsv->bhdv', kc * jnp.exp(last - cum), vc))
        return state, o

    state, o = lax.scan(step, s0, (chunks(q), chunks(k), chunks(v), chunks(g)))
    return o.transpose(1, 0, 3, 2, 4).reshape(b, t, h, dv), state


def gla_recurrent(q, k, v, g, s0):
    def step(state, inp):
        qt, kt, vt, gt = inp
        state = jnp.exp(gt)[..., None] * state + kt[..., None] * vt[:, :, None, :]
        return state, jnp.einsum('bhd,bhdv->bhv', qt, state)

    def tmajor(z):
        return z.transpose(1, 0, 2, 3)

    state, o = lax.scan(step, s0, (tmajor(q), tmajor(k), tmajor(v), tmajor(g)))
    return tmajor(o), state


def merge_and_ffn(x, oa, ob, rb, ga, gb, g_gla, w_pa, w_pb, w_o, g_norm2, w_up, w_down):
    b, t, _ = x.shape
    ob = rmsnorm(ob, g_gla) * jax.nn.silu(rb.reshape(b, t, GLA_HEADS, GLA_HEAD_DV).astype(jnp.float32))
    ob = ob.reshape(b, t, GLA_DV).astype(x.dtype)
    mix = jax.nn.sigmoid(ga) * (oa @ w_pa) + jax.nn.sigmoid(gb) * (ob @ w_pb)
    h = x + mix @ w_o
    f = jnp.square(jax.nn.relu(rmsnorm(h, g_norm2) @ w_up))
    return h + f @ w_down


def setup_inputs(seed: int = 0) -> dict:
    key = jax.random.key(seed)
    ks = jax.random.split(key, 24)

    def nrm(k, shape, scale):
        return jax.random.normal(k, shape, jnp.float32) * scale

    buf = [min(w, PAST_LEN) for w, _ in A_GROUPS]
    hg, hd = A_HEADS_PER_GROUP, A_HEAD_DIM
    return {
        'x_prompt': nrm(ks[0], (BATCH, SEQ, D_MODEL), 1.0),
        'x_sample': nrm(ks[1], (DEC_BATCH, DEC_SEQ, D_MODEL), 1.0),
        'cache_a1_k': nrm(ks[2], (DEPTH, DEC_BATCH, buf[0], hg, hd), 1.0),
        'cache_a1_v': nrm(ks[3], (DEPTH, DEC_BATCH, buf[0], hg, hd), 1.0),
        'cache_a2_k': nrm(ks[4], (DEPTH, DEC_BATCH, buf[1], hg, hd), 1.0),
        'cache_a2_v': nrm(ks[5], (DEPTH, DEC_BATCH, buf[1], hg, hd), 1.0),
        'cache_a3_k': nrm(ks[6], (DEPTH, DEC_BATCH, buf[2], hg, hd), 1.0),
        'cache_a3_v': nrm(ks[7], (DEPTH, DEC_BATCH, buf[2], hg, hd), 1.0),
        'state_gla': nrm(ks[8], (DEPTH, DEC_BATCH, GLA_HEADS, GLA_HEAD_DK, GLA_HEAD_DV), 0.1),
        'g_norm1': 1.0 + nrm(ks[9], (DEPTH, D_MODEL), 0.01),
        'w_in': nrm(ks[10], (DEPTH, D_MODEL, IN_WIDTH), D_MODEL ** -0.5),
        'w_gk2': nrm(ks[11], (DEPTH, GLA_GATE_RANK, GLA_DK), GLA_GATE_RANK ** -0.5),
        'b_gk': nrm(ks[12], (DEPTH, GLA_DK), 0.1),
        'g_gla': 1.0 + nrm(ks[13], (DEPTH, GLA_HEAD_DV), 0.01),
        'w_pa': nrm(ks[14], (DEPTH, A_OUT, D_MODEL), A_OUT ** -0.5),
        'w_pb': nrm(ks[15], (DEPTH, GLA_DV, D_MODEL), GLA_DV ** -0.5),
        'w_o': nrm(ks[16], (DEPTH, D_MODEL, D_MODEL), D_MODEL ** -0.5),
        'g_norm2': 1.0 + nrm(ks[17], (DEPTH, D_MODEL), 0.01),
        'w_up': nrm(ks[18], (DEPTH, D_MODEL, D_FF), D_MODEL ** -0.5),
        'w_down': nrm(ks[19], (DEPTH, D_FF, D_MODEL), D_FF ** -0.5),
        'g_final': 1.0 + nrm(ks[20], (D_MODEL,), 0.01),
    }


def reference(x_prompt, x_sample, cache_a1_k, cache_a1_v, cache_a2_k, cache_a2_v, cache_a3_k, cache_a3_v,
              state_gla, g_norm1, w_in, w_gk2, b_gk, g_gla, w_pa, w_pb, w_o, g_norm2, w_up, w_down, g_final):
    pos_p = jnp.arange(x_prompt.shape[1], dtype=jnp.int32)
    pos_s = PAST_LEN + jnp.arange(x_sample.shape[1], dtype=jnp.int32)
    caches = (cache_a1_k, cache_a1_v, cache_a2_k, cache_a2_v, cache_a3_k, cache_a3_v)
    xp, xs = x_prompt, x_sample
    p_new = [[] for _ in range(7)]
    s_new = [[] for _ in range(7)]
    for l in range(DEPTH):
        qa, ka, va, qb, kb, vb, gk, rb, ga, gb = project(rmsnorm(xp, g_norm1[l]), w_in[l], w_gk2[l], b_gk[l], pos_p)
        oa = dilated_prompt(qa, ka, va)
        s0 = jnp.zeros((xp.shape[0], GLA_HEADS, GLA_HEAD_DK, GLA_HEAD_DV), jnp.float32)
        ob, s_fin = gla_chunked(qb, kb, vb, gk, s0)
        xp = merge_and_ffn(xp, oa, ob, rb, ga, gb, g_gla[l], w_pa[l], w_pb[l], w_o[l], g_norm2[l], w_up[l], w_down[l])
        for j, z in enumerate(prompt_buffers(ka, va) + [s_fin.astype(state_gla.dtype)]):
            p_new[j].append(z)
        qa, ka, va, qb, kb, vb, gk, rb, ga, gb = project(rmsnorm(xs, g_norm1[l]), w_in[l], w_gk2[l], b_gk[l], pos_s)
        oa, bufs = dilated_sample(qa, ka, va, [c[l] for c in caches])
        ob, s_upd = gla_recurrent(qb, kb, vb, gk, state_gla[l].astype(jnp.float32))
        xs = merge_and_ffn(xs, oa, ob, rb, ga, gb, g_gla[l], w_pa[l], w_pb[l], w_o[l], g_norm2[l], w_up[l], w_down[l])
        for j, z in enumerate(bufs + [s_upd.astype(state_gla.dtype)]):
            s_new[j].append(z)
    y_prompt = rmsnorm(xp, g_final)
    y_sample = rmsnorm(xs, g_final)
    p_a1_k, p_a1_v, p_a2_k, p_a2_v, p_a3_k, p_a3_v, p_gla = [jnp.stack(z) for z in p_new]
    s_a1_k, s_a1_v, s_a2_k, s_a2_v, s_a3_k, s_a3_v, s_gla = [jnp.stack(z) for z in s_new]
    return (y_prompt, y_sample, p_a1_k, p_a1_v, p_a2_k, p_a2_v, p_a3_k, p_a3_v, p_gla,
            s_a1_k, s_a1_v, s_a2_k, s_a2_v, s_a3_k, s_a3_v, s_gla)
```

```python
import functools

import jax
import jax.numpy as jnp
import numpy as np
from jax import lax
from jax.experimental import pallas as pl
from jax.experimental.pallas import tpu as pltpu

D_MODEL = 1024
PAST_LEN = 16384
A_GROUPS = ((128, 1), (512, 4), (2048, 16))
A_N_GROUPS = 3
A_HEADS_PER_GROUP = 4
A_HEAD_DIM = 64
A_GROUP_WIDTH = A_HEADS_PER_GROUP * A_HEAD_DIM
A_WIDTH = A_N_GROUPS * A_GROUP_WIDTH
A_WINDOW_KEYS = 128
ROPE_THETA = 10000.0
GLA_HEADS = 4
GLA_HEAD_DK = 128
GLA_HEAD_DV = 256
GLA_DK = GLA_HEADS * GLA_HEAD_DK
GLA_DV = GLA_HEADS * GLA_HEAD_DV
GLA_GATE_RANK = 16
GLA_GATE_NORM = 16.0
D_FF = 4 * D_MODEL
EPS = 1e-6

LANES = 128
V7X_SCOPED_VMEM_CAP = 56 * 1024 * 1024

ROW_TILE = 512
Q_BLOCK = 128
GLA_CHUNK = 64
GLA_ROW_TILE = 512
FF_CHUNK = 1024

_F32 = jnp.float32
_BF16 = jnp.bfloat16
_NEG_INF = float("-inf")


def _vmem_limit(nbytes):
    return int(min(V7X_SCOPED_VMEM_CAP, max(16 * 1024 * 1024, nbytes * 3 // 2)))


def _resident(shape):
    nd = len(shape)
    return pl.BlockSpec(shape, lambda *_: (0,) * nd, pipeline_mode=pl.Buffered(1))


def _rms_normed(x, g):
    return x * lax.rsqrt(jnp.mean(x * x, axis=-1, keepdims=True) + EPS) * g


def _rope_table_kernel(inv_ref, sign_ref, cos_ref, sin_ref, *, pos0, pos_step):
    rows = cos_ref.shape[0]
    row = lax.broadcasted_iota(jnp.int32, (rows, LANES), 0) + pl.program_id(0) * rows
    pos = (pos0 + row * pos_step).astype(_F32)
    ang = pos * inv_ref[...]
    cos_ref[...] = jnp.cos(ang)
    sin_ref[...] = jnp.sin(ang) * sign_ref[...]


def _rope_tables(n_rows, pos0, pos_step):
    half = A_HEAD_DIM // 2
    lane = jnp.arange(LANES)
    inv = ROPE_THETA ** (-((lane % half).astype(_F32)) / half)
    sign = jnp.where((lane % A_HEAD_DIM) < half, -1.0, 1.0).astype(_F32)
    tile = min(n_rows, ROW_TILE)
    return pl.pallas_call(
        functools.partial(_rope_table_kernel, pos0=pos0, pos_step=pos_step),
        grid=(n_rows // tile,),
        in_specs=[pl.BlockSpec((1, LANES), lambda i: (0, 0))] * 2,
        out_specs=[pl.BlockSpec((tile, LANES), lambda i: (i, 0))] * 2,
        out_shape=[jax.ShapeDtypeStruct((n_rows, LANES), _F32)] * 2,
        name="rope_table",
    )(inv[None, :], sign[None, :])


def _rope_slab(z, cos, sin_signed, first_half):
    half = A_HEAD_DIM // 2
    partner = jnp.where(first_half, pltpu.roll(z, LANES - half, 1), pltpu.roll(z, half, 1))
    return z * cos + partner * sin_signed


_N_SLABS = 3 * A_WIDTH // LANES
_SLABS_PER_SECTION = A_WIDTH // LANES
_SLABS_PER_GROUP = A_GROUP_WIDTH // LANES


def _attn_proj_slabs(x_ref, g_ref, w_ref, cos_ref, sin_ref):
    u = _rms_normed(x_ref[0], g_ref[...]).astype(_BF16)
    y = jnp.dot(u, w_ref[...], preferred_element_type=_F32)
    cos = cos_ref[...]
    sin = sin_ref[...]
    lane = lax.broadcasted_iota(jnp.int32, cos.shape, 1)
    first_half = (lane % A_HEAD_DIM) < (A_HEAD_DIM // 2)
    for s in range(_N_SLABS):
        z = y[:, s * LANES:(s + 1) * LANES]
        if s < _SLABS_PER_SECTION:
            z = _rope_slab(z, cos, sin, first_half) * (A_HEAD_DIM ** -0.5)
        elif s < 2 * _SLABS_PER_SECTION:
            z = _rope_slab(z, cos, sin, first_half)
        yield s, z


def _attn_proj_prompt_kernel(x_ref, g_ref, w_ref, cos_ref, sin_ref,
                             p1_ref, p2_ref, p3_ref, c1k_ref, c1v_ref, c2k_ref, c2v_ref, c3k_ref, c3v_ref,
                             slab_ref, *, tm, n_tiles):
    i = pl.program_id(1)
    packs = (p1_ref, p2_ref, p3_ref)
    caches = ((c1k_ref, c1v_ref), (c2k_ref, c2v_ref), (c3k_ref, c3v_ref))
    for s, z in _attn_proj_slabs(x_ref, g_ref, w_ref, cos_ref, sin_ref):
        slab_ref[s] = z
    for s in range(_N_SLABS):
        section, within = divmod(s, _SLABS_PER_SECTION)
        group, pair = divmod(within, _SLABS_PER_GROUP)
        dil = A_GROUPS[group][1]
        col = (section * _SLABS_PER_GROUP + pair) * LANES
        for r in range(dil):
            rows = slab_ref[s, pl.ds(r, tm // dil, stride=dil), :] if dil > 1 else slab_ref[s]
            packs[group][0, r, :, col:col + LANES] = rows.astype(_BF16)

    for group, (window, _) in enumerate(A_GROUPS):
        keep = min(window, tm)
        first_kept_tile = n_tiles - max(window // tm, 1)

        @pl.when(i >= first_kept_tile)
        def _(group=group, keep=keep):
            for section in (1, 2):
                for pair in range(_SLABS_PER_GROUP):
                    s = section * _SLABS_PER_SECTION + group * _SLABS_PER_GROUP + pair
                    caches[group][section - 1][0, :, pair * LANES:(pair + 1) * LANES] = slab_ref[s, tm - keep:tm, :]


def _attn_proj_sample_kernel(x_ref, g_ref, w_ref, cos_ref, sin_ref, o_ref):
    for s, z in _attn_proj_slabs(x_ref, g_ref, w_ref, cos_ref, sin_ref):
        o_ref[0, :, s * LANES:(s + 1) * LANES] = z


def _attn_proj_prompt(x, g, w, cos, sin):
    b, t, d = x.shape
    tm = ROW_TILE
    n_tiles = t // tm
    pack_shapes, pack_specs = [], []
    for _, dil in A_GROUPS:
        pack_shapes.append(jax.ShapeDtypeStruct((b, dil, t // dil, 3 * A_GROUP_WIDTH), _BF16))
        pack_specs.append(pl.BlockSpec((1, dil, tm // dil, 3 * A_GROUP_WIDTH), lambda bi, i: (bi, 0, i, 0)))
    cache_shapes, cache_specs = [], []
    for window, _ in A_GROUPS:
        keep = min(window, t)
        blk = min(keep, tm)
        first = n_tiles - max(keep // tm, 1)
        for _ in range(2):
            cache_shapes.append(jax.ShapeDtypeStruct((b, keep, A_GROUP_WIDTH), _F32))
            cache_specs.append(pl.BlockSpec(
                (1, blk, A_GROUP_WIDTH), lambda bi, i, first=first: (bi, jnp.maximum(i - first, 0), 0)))
    est = (w.size * 2 + 2 * tm * d * 4 + 2 * 3 * tm * 3 * A_GROUP_WIDTH * 2 + 2 * 6 * tm * A_GROUP_WIDTH * 4
           + 2 * tm * 3 * A_WIDTH * 4)
    return pl.pallas_call(
        functools.partial(_attn_proj_prompt_kernel, tm=tm, n_tiles=n_tiles),
        grid=(b, n_tiles),
        in_specs=[
            pl.BlockSpec((1, tm, d), lambda bi, i: (bi, i, 0)),
            _resident((1, d)),
            _resident(w.shape),
            pl.BlockSpec((tm, LANES), lambda bi, i: (i, 0)),
            pl.BlockSpec((tm, LANES), lambda bi, i: (i, 0)),
        ],
        out_specs=pack_specs + cache_specs,
        out_shape=pack_shapes + cache_shapes,
        scratch_shapes=[pltpu.VMEM((_N_SLABS, tm, LANES), _F32)],
        compiler_params=pltpu.CompilerParams(
            dimension_semantics=("arbitrary", "arbitrary"), vmem_limit_bytes=_vmem_limit(est)),
        name="attn_proj_prompt",
    )(x, g, w, cos, sin)


def _attn_proj_sample(x, g, w, cos, sin):
    _, m, d = x.shape
    est = w.size * 2 * 2 + 4 * m * 3 * A_WIDTH * 4
    return pl.pallas_call(
        _attn_proj_sample_kernel,
        grid=(1,),
        in_specs=[
            pl.BlockSpec((1, m, d), lambda i: (0, 0, 0)),
            pl.BlockSpec((1, d), lambda i: (0, 0)),
            pl.BlockSpec(w.shape, lambda i: (0, 0)),
            pl.BlockSpec((m, LANES), lambda i: (0, 0)),
            pl.BlockSpec((m, LANES), lambda i: (0, 0)),
        ],
        out_specs=pl.BlockSpec((1, m, 3 * A_WIDTH), lambda i: (0, 0, 0)),
        out_shape=jax.ShapeDtypeStruct((1, m, 3 * A_WIDTH), _F32),
        compiler_params=pltpu.CompilerParams(vmem_limit_bytes=_vmem_limit(est)),
        name="attn_proj_sample",
    )(x, g, w, cos, sin)


_GP_SECTIONS = {}
_off = 0
for _name, _width in (("qb", GLA_DK), ("kb", GLA_DK), ("vb", GLA_DV), ("rb", GLA_DV),
                      ("ga", D_MODEL), ("gb", D_MODEL), ("glr", LANES)):
    _GP_SECTIONS[_name] = (_off, _width)
    _off += _width
_GP_WIDTH = _off


def _gla_proj_kernel(x_ref, g_ref, w_ref, wgk_ref, bgk_ref,
                     qb_ref, kb_ref, vb_ref, rb_ref, ga_ref, gb_ref, gk_ref):
    u = _rms_normed(x_ref[...], g_ref[...]).astype(_BF16)

    def section(name):
        off, width = _GP_SECTIONS[name]
        return jnp.dot(u, w_ref[:, off:off + width], preferred_element_type=_F32)

    qb_ref[...] = (section("qb") * (GLA_HEAD_DK ** -0.5)).astype(qb_ref.dtype)
    kb_ref[...] = section("kb").astype(kb_ref.dtype)
    vb_ref[...] = section("vb").astype(vb_ref.dtype)
    rb_ref[...] = section("rb").astype(rb_ref.dtype)
    ga_ref[...] = section("ga").astype(ga_ref.dtype)
    gb_ref[...] = section("gb").astype(gb_ref.dtype)
    glr = section("glr").astype(_BF16)
    z = jnp.dot(glr, wgk_ref[...], preferred_element_type=_F32) + bgk_ref[...]
    log_sigmoid = jnp.minimum(z, 0.0) - jnp.log1p(jnp.exp(-jnp.abs(z)))
    gk_ref[...] = log_sigmoid / GLA_GATE_NORM


def _gla_proj(x, g, w, wgk, bgk, tm, act_dtype):
    m, d = x.shape
    widths = (GLA_DK, GLA_DK, GLA_DV, GLA_DV, D_MODEL, D_MODEL)
    out_shapes = [jax.ShapeDtypeStruct((m, wd), act_dtype) for wd in widths]
    out_shapes.append(jax.ShapeDtypeStruct((m, GLA_DK), _F32))
    out_specs = [pl.BlockSpec((tm, wd), lambda i: (i, 0)) for wd in widths + (GLA_DK,)]
    est = w.size * 2 + 2 * tm * d * 4 + 2 * tm * (sum(widths) + GLA_DK) * 4 + tm * GLA_DV * 4 * 2
    return pl.pallas_call(
        _gla_proj_kernel,
        grid=(m // tm,),
        in_specs=[
            pl.BlockSpec((tm, d), lambda i: (i, 0)),
            _resident((1, d)),
            _resident(w.shape),
            _resident(wgk.shape),
            _resident(bgk.shape),
        ],
        out_specs=out_specs,
        out_shape=out_shapes,
        compiler_params=pltpu.CompilerParams(
            dimension_semantics=("arbitrary",), vmem_limit_bytes=_vmem_limit(est)),
        name="gla_proj",
    )(x, g, w, wgk, bgk)


def _dilated_attn_kernel(q1_ref, k1_ref, v1_ref, q2_ref, k2_ref, v2_ref, q3_ref, k3_ref, v3_ref,
                         o_ref, m_ref, l_ref, n_ref):
    qb = Q_BLOCK
    lane = lax.broadcasted_iota(jnp.int32, (qb, LANES), 1)
    head0 = lane < A_HEAD_DIM
    row = lax.broadcasted_iota(jnp.int32, (qb, 2 * qb), 0)
    col = lax.broadcasted_iota(jnp.int32, (qb, 2 * qb), 1)
    rel = row - col
    groups = ((q1_ref, k1_ref, v1_ref), (q2_ref, k2_ref, v2_ref), (q3_ref, k3_ref, v3_ref))

    for group, (q_ref, k_ref, v_ref) in enumerate(groups):
        dil = A_GROUPS[group][1]
        n_blocks = q_ref.shape[2] // qb

        def block(idx, carry, q_ref=q_ref, k_ref=k_ref, v_ref=v_ref, dil=dil, n_blocks=n_blocks, group=group):
            r = idx // n_blocks
            n = idx % n_blocks
            q0 = pl.multiple_of(n * qb, qb)
            k0 = pl.multiple_of(jnp.maximum(n - 1, 0) * qb, qb)
            q = q_ref[0, r, pl.ds(q0, qb), :]
            k = k_ref[0, r, pl.ds(k0, 2 * qb), :]
            v = v_ref[0, r, pl.ds(k0, 2 * qb), :]
            diff = rel + (q0 - k0)
            valid = (diff >= 0) & (diff <= A_WINDOW_KEYS)
            stats = []
            for own_lanes in (head0, jnp.logical_not(head0)):
                qh = jnp.where(own_lanes, q, jnp.zeros_like(q))
                s = lax.dot_general(qh, k, (((1,), (1,)), ((), ())), preferred_element_type=_F32)
                s = jnp.where(valid, s, _NEG_INF)
                m = jnp.max(s, axis=1, keepdims=True)
                p = jnp.exp(s - m)
                l = jnp.sum(p, axis=1, keepdims=True)
                num = jnp.dot(p.astype(_BF16), v, preferred_element_type=_F32)
                stats.append((m, l, num))
            m_g = jnp.where(head0, stats[0][0], stats[1][0])
            l_g = jnp.where(head0, stats[0][1], stats[1][1])
            n_g = jnp.where(head0, stats[0][2], stats[1][2])
            if dil == 1:
                rows = pl.ds(q0, qb)
            else:
                rows = pl.ds(q0 * dil + r, qb, stride=dil)
            if group == 0:
                m_ref[rows, :] = m_g
                l_ref[rows, :] = l_g
                n_ref[rows, :] = n_g
            else:
                m_old = m_ref[rows, :]
                m_new = jnp.maximum(m_old, m_g)
                a = jnp.exp(m_old - m_new)
                b = jnp.exp(m_g - m_new)
                m_ref[rows, :] = m_new
                l_ref[rows, :] = a * l_ref[rows, :] + b * l_g
                n_ref[rows, :] = a * n_ref[rows, :] + b * n_g
            return carry

        lax.fori_loop(0, dil * n_blocks, block, 0)

    o_ref[0] = (n_ref[...] / l_ref[...]).astype(o_ref.dtype)


def _dilated_attn(packs, t):
    b = packs[0].shape[0]
    n_pairs = A_GROUP_WIDTH // LANES
    in_specs, operands = [], []
    for pack, (_, dil) in zip(packs, A_GROUPS):
        for section in range(3):
            in_specs.append(pl.BlockSpec(
                (1, dil, t // dil, LANES), lambda bi, p, section=section: (bi, 0, 0, section * n_pairs + p)))
            operands.append(pack)
    est = 2 * 9 * t * LANES * 2 + 2 * t * LANES * 2 + 3 * t * LANES * 4
    return pl.pallas_call(
        _dilated_attn_kernel,
        grid=(b, n_pairs),
        in_specs=in_specs,
        out_specs=pl.BlockSpec((1, t, LANES), lambda bi, p: (bi, 0, p)),
        out_shape=jax.ShapeDtypeStruct((b, t, A_GROUP_WIDTH), _BF16),
        scratch_shapes=[pltpu.VMEM((t, LANES), _F32)] * 3,
        compiler_params=pltpu.CompilerParams(
            dimension_semantics=("arbitrary", "arbitrary"), vmem_limit_bytes=_vmem_limit(est)),
        name="dilated_attn",
    )(*operands)


def _gla_chunked_kernel(q_ref, k_ref, v_ref, gk_ref, r_ref, gn_ref, o_ref, fin_ref, st_ref):
    c = GLA_CHUNK
    tile = pl.program_id(1)

    @pl.when(tile == 0)
    def _():
        st_ref[...] = jnp.zeros_like(st_ref)

    row = lax.broadcasted_iota(jnp.int32, (c, c), 0)
    col = lax.broadcasted_iota(jnp.int32, (c, c), 1)
    causal = row >= col
    causal_f = causal.astype(_F32)
    gn = gn_ref[...]

    def chunk(ci, carry):
        r0 = pl.multiple_of(ci * c, c)
        for h in range(GLA_HEADS):
            dk = slice(h * GLA_HEAD_DK, (h + 1) * GLA_HEAD_DK)
            dv = slice(h * GLA_HEAD_DV, (h + 1) * GLA_HEAD_DV)
            g = gk_ref[0, pl.ds(r0, c), dk]
            cum = jnp.dot(causal_f, g, preferred_element_type=_F32, precision=lax.Precision.HIGHEST)
            q = q_ref[0, pl.ds(r0, c), dk].astype(_F32)
            k = k_ref[0, pl.ds(r0, c), dk].astype(_F32)
            v = v_ref[0, pl.ds(r0, c), dv]
            mid = cum[c // 2 - 1:c // 2, :]
            last = cum[c - 1:c, :]
            q_mid = (q * jnp.exp(cum - mid)).astype(_BF16)
            k_mid = (k * jnp.exp(mid - cum)).astype(_BF16)
            scores = lax.dot_general(q_mid, k_mid, (((1,), (1,)), ((), ())), preferred_element_type=_F32)
            scores = jnp.where(causal, scores, 0.0).astype(_BF16)
            o = jnp.dot(scores, v, preferred_element_type=_F32)
            state_t = st_ref[h]
            q_in = (q * jnp.exp(cum)).astype(_BF16)
            o = o + lax.dot_general(q_in, state_t.astype(_BF16), (((1,), (1,)), ((), ())),
                                    preferred_element_type=_F32)
            k_out = (k * jnp.exp(last - cum)).astype(_BF16)
            update_t = lax.dot_general(v, k_out, (((0,), (0,)), ((), ())), preferred_element_type=_F32)
            st_ref[h] = state_t * jnp.exp(last) + update_t
            rb = r_ref[0, pl.ds(r0, c), dv].astype(_F32)
            o = _rms_normed(o, gn) * (rb * jax.nn.sigmoid(rb))
            o_ref[0, pl.ds(r0, c), dv] = o.astype(o_ref.dtype)
        return carry

    lax.fori_loop(0, q_ref.shape[1] // c, chunk, 0)

    @pl.when(tile == pl.num_programs(1) - 1)
    def _():
        for h in range(GLA_HEADS):
            fin_ref[0, h] = st_ref[h].T


def _gla_chunked(qb, kb, vb, gk, rb, gn):
    b, t, _ = qb.shape
    tm = GLA_ROW_TILE
    est = 2 * tm * (2 * GLA_DK * 2 + 2 * GLA_DV * 2 + GLA_DK * 4 + GLA_DV * 2) + 3 * GLA_HEADS * GLA_HEAD_DK * GLA_HEAD_DV * 4
    return pl.pallas_call(
        _gla_chunked_kernel,
        grid=(b, t // tm),
        in_specs=[
            pl.BlockSpec((1, tm, GLA_DK), lambda bi, i: (bi, i, 0)),
            pl.BlockSpec((1, tm, GLA_DK), lambda bi, i: (bi, i, 0)),
            pl.BlockSpec((1, tm, GLA_DV), lambda bi, i: (bi, i, 0)),
            pl.BlockSpec((1, tm, GLA_DK), lambda bi, i: (bi, i, 0)),
            pl.BlockSpec((1, tm, GLA_DV), lambda bi, i: (bi, i, 0)),
            pl.BlockSpec((1, GLA_HEAD_DV), lambda bi, i: (0, 0)),
        ],
        out_specs=[
            pl.BlockSpec((1, tm, GLA_DV), lambda bi, i: (bi, i, 0)),
            pl.BlockSpec((1, GLA_HEADS, GLA_HEAD_DK, GLA_HEAD_DV), lambda bi, i: (bi, 0, 0, 0)),
        ],
        out_shape=[
            jax.ShapeDtypeStruct((b, t, GLA_DV), _BF16),
            jax.ShapeDtypeStruct((b, GLA_HEADS, GLA_HEAD_DK, GLA_HEAD_DV), _F32),
        ],
        scratch_shapes=[pltpu.VMEM((GLA_HEADS, GLA_HEAD_DV, GLA_HEAD_DK), _F32)],
        compiler_params=pltpu.CompilerParams(
            dimension_semantics=("arbitrary", "arbitrary"), vmem_limit_bytes=_vmem_limit(est)),
        name="gla_chunked",
    )(qb, kb, vb, gk, rb, gn)


def _sample_attn_kernel(qkv_ref, c1k_ref, c1v_ref, c2k_ref, c2v_ref, c3k_ref, c3v_ref,
                        o_ref, n1k_ref, n1v_ref, n2k_ref, n2v_ref, n3k_ref, n3v_ref):
    caches = ((c1k_ref, c1v_ref, n1k_ref, n1v_ref), (c2k_ref, c2v_ref, n2k_ref, n2v_ref),
              (c3k_ref, c3v_ref, n3k_ref, n3v_ref))
    n_heads = A_HEADS_PER_GROUP
    head_rows = 8
    hrow = lax.broadcasted_iota(jnp.int32, (head_rows, A_GROUP_WIDTH), 0)
    hlane = lax.broadcasted_iota(jnp.int32, (head_rows, A_GROUP_WIDTH), 1)
    own = (hlane // A_HEAD_DIM) == hrow
    stats = []
    for group, (kc_ref, vc_ref, kn_ref, vn_ref) in enumerate(caches):
        window, dil = A_GROUPS[group]
        base = group * A_GROUP_WIDTH
        q = qkv_ref[0, :, base:base + A_GROUP_WIDTH]
        k_new = qkv_ref[0, :, A_WIDTH + base:A_WIDTH + base + A_GROUP_WIDTH]
        v_new = qkv_ref[0, :, 2 * A_WIDTH + base:2 * A_WIDTH + base + A_GROUP_WIDTH]
        kc = kc_ref[0]
        vc = vc_ref[0]
        crow = lax.broadcasted_iota(jnp.int32, (window, A_GROUP_WIDTH), 0)
        is_last = crow == window - 1
        kn_ref[0] = jnp.where(is_last, k_new, pltpu.roll(kc, window - 1, 0))
        vn_ref[0] = jnp.where(is_last, v_new, pltpu.roll(vc, window - 1, 0))

        q_heads = jnp.where(own, q, 0.0)
        s_old = lax.dot_general(q_heads.astype(_BF16), kc.astype(_BF16), (((1,), (1,)), ((), ())),
                                preferred_element_type=_F32)
        pos = lax.broadcasted_iota(jnp.int32, (head_rows, window), 1)
        s_old = jnp.where(pos % dil == 0, s_old, _NEG_INF)
        s_new = jnp.sum(q_heads * k_new, axis=1, keepdims=True)
        m = jnp.maximum(jnp.max(s_old, axis=1, keepdims=True), s_new)
        p_old = jnp.exp(s_old - m)
        p_new = jnp.exp(s_new - m)
        l = jnp.sum(p_old, axis=1, keepdims=True) + p_new
        num = jnp.dot(p_old.astype(_BF16), vc.astype(_BF16), preferred_element_type=_F32) + p_new * v_new
        stats.append((m, l, num))
    m_all = jnp.maximum(jnp.maximum(stats[0][0], stats[1][0]), stats[2][0])
    num = sum(jnp.exp(m - m_all) * n for m, _, n in stats)
    den = sum(jnp.exp(m - m_all) * l for m, l, _ in stats)
    o = jnp.where(own, num / den, 0.0)
    del n_heads
    o_ref[0] = jnp.sum(o, axis=0, keepdims=True)


def _sample_attn(qkv, caches):
    nb = qkv.shape[0]
    in_specs = [pl.BlockSpec((1, 1, 3 * A_WIDTH), lambda i: (i, 0, 0))]
    out_specs = [pl.BlockSpec((1, 1, A_GROUP_WIDTH), lambda i: (i, 0, 0))]
    out_shapes = [jax.ShapeDtypeStruct((nb, 1, A_GROUP_WIDTH), _F32)]
    est = 0
    for c in caches:
        spec = pl.BlockSpec((1,) + c.shape[1:], lambda i: (i, 0, 0))
        in_specs.append(spec)
        out_specs.append(spec)
        out_shapes.append(jax.ShapeDtypeStruct(c.shape, c.dtype))
        est += 6 * c.shape[1] * c.shape[2] * 4
    return pl.pallas_call(
        _sample_attn_kernel,
        grid=(nb,),
        in_specs=in_specs,
        out_specs=out_specs,
        out_shape=out_shapes,
        compiler_params=pltpu.CompilerParams(
            dimension_semantics=("arbitrary",), vmem_limit_bytes=_vmem_limit(est)),
        name="sample_attn",
    )(qkv, *caches)


def _as_column(row_vec):
    n = row_vec.shape[1]
    eye = lax.broadcasted_iota(jnp.int32, (n, n), 0) == lax.broadcasted_iota(jnp.int32, (n, n), 1)
    return jnp.sum(jnp.where(eye, row_vec, 0.0), axis=1, keepdims=True)


def _sample_gla_kernel(q_ref, k_ref, v_ref, gk_ref, r_ref, gn_ref, st_ref, o_ref, new_ref):
    gn = gn_ref[...]
    for h in range(GLA_HEADS):
        dk = slice(h * GLA_HEAD_DK, (h + 1) * GLA_HEAD_DK)
        dv = slice(h * GLA_HEAD_DV, (h + 1) * GLA_HEAD_DV)
        decay = _as_column(jnp.exp(gk_ref[0, :, dk]))
        k_col = _as_column(k_ref[0, :, dk])
        q_col = _as_column(q_ref[0, :, dk])
        state = decay * st_ref[0, h] + k_col * v_ref[0, :, dv]
        new_ref[0, h] = state
        o = jnp.sum(q_col * state, axis=0, keepdims=True)
        rb = r_ref[0, :, dv]
        o_ref[0, :, dv] = _rms_normed(o, gn) * (rb * jax.nn.sigmoid(rb))


def _sample_gla(qb, kb, vb, gk, rb, gn, state):
    nb = qb.shape[0]

    def row_spec(width):
        return pl.BlockSpec((1, 1, width), lambda i: (i, 0, 0))

    state_spec = pl.BlockSpec((1,) + state.shape[1:], lambda i: (i, 0, 0, 0))
    return pl.pallas_call(
        _sample_gla_kernel,
        grid=(nb,),
        in_specs=[row_spec(GLA_DK), row_spec(GLA_DK), row_spec(GLA_DV), row_spec(GLA_DK), row_spec(GLA_DV),
                  pl.BlockSpec((1, GLA_HEAD_DV), lambda i: (0, 0)), state_spec],
        out_specs=[row_spec(GLA_DV), state_spec],
        out_shape=[jax.ShapeDtypeStruct((nb, 1, GLA_DV), _F32), jax.ShapeDtypeStruct(state.shape, state.dtype)],
        compiler_params=pltpu.CompilerParams(dimension_semantics=("arbitrary",)),
        name="sample_gla",
    )(qb, kb, vb, gk, rb, gn, state)


def _merge_proj_kernel(x_ref, oa_ref, ob_ref, ga_ref, gb_ref, wpa_ref, wpb_ref, wo_ref, h_ref):
    ya = jnp.dot(oa_ref[...].astype(_BF16), wpa_ref[...], preferred_element_type=_F32)
    yb = jnp.dot(ob_ref[...].astype(_BF16), wpb_ref[...], preferred_element_type=_F32)
    mix = jax.nn.sigmoid(ga_ref[...].astype(_F32)) * ya + jax.nn.sigmoid(gb_ref[...].astype(_F32)) * yb
    h_ref[...] = x_ref[...] + jnp.dot(mix.astype(_BF16), wo_ref[...], preferred_element_type=_F32)


def _merge_proj(x, oa, ob, ga, gb, wpa, wpb, wo, tm):
    m, d = x.shape

    def rows(width):
        return pl.BlockSpec((tm, width), lambda i: (i, 0))

    est = (wpa.size + wpb.size + wo.size) * 2 + 2 * tm * (2 * d * 4 + A_GROUP_WIDTH * 4 + 3 * d * 4) + 4 * tm * d * 4
    return pl.pallas_call(
        _merge_proj_kernel,
        grid=(m // tm,),
        in_specs=[rows(d), rows(A_GROUP_WIDTH), rows(GLA_DV), rows(d), rows(d),
                  _resident(wpa.shape), _resident(wpb.shape), _resident(wo.shape)],
        out_specs=rows(d),
        out_shape=jax.ShapeDtypeStruct((m, d), _F32),
        compiler_params=pltpu.CompilerParams(
            dimension_semantics=("arbitrary",), vmem_limit_bytes=_vmem_limit(est)),
        name="merge_proj",
    )(x, oa, ob, ga, gb, wpa, wpb, wo)


def _ffn_final_kernel(h_ref, g2_ref, wup_ref, wdn_ref, gf_ref, y_ref):
    h = h_ref[...]
    hn = _rms_normed(h, g2_ref[...]).astype(_BF16)
    acc = h
    for c0 in range(0, D_FF, FF_CHUNK):
        f = jnp.maximum(jnp.dot(hn, wup_ref[:, c0:c0 + FF_CHUNK], preferred_element_type=_F32), 0.0)
        acc = acc + jnp.dot((f * f).astype(_BF16), wdn_ref[c0:c0 + FF_CHUNK, :], preferred_element_type=_F32)
    y_ref[...] = _rms_normed(acc, gf_ref[...])


def _ffn_final(h, g2, wup, wdn, gf, tm):
    m, d = h.shape
    est = (wup.size + wdn.size) * 2 + 4 * tm * d * 4 + 3 * tm * FF_CHUNK * 4 + 2 * tm * d * 4
    return pl.pallas_call(
        _ffn_final_kernel,
        grid=(m // tm,),
        in_specs=[pl.BlockSpec((tm, d), lambda i: (i, 0)), _resident((1, d)),
                  _resident(wup.shape), _resident(wdn.shape), _resident((1, d))],
        out_specs=pl.BlockSpec((tm, d), lambda i: (i, 0)),
        out_shape=jax.ShapeDtypeStruct((m, d), _F32),
        compiler_params=pltpu.CompilerParams(
            dimension_semantics=("arbitrary",), vmem_limit_bytes=_vmem_limit(est)),
        name="ffn_final",
    )(h, g2, wup, wdn, gf)


def _split_w_in(w_in):
    sizes = (A_WIDTH, A_WIDTH, A_WIDTH, GLA_DK, GLA_DK, GLA_DV, GLA_DV, GLA_GATE_RANK, D_MODEL, D_MODEL)
    offs = np.concatenate([[0], np.cumsum(sizes)])
    qa, ka, va, qb, kb, vb, rb, glr, ga, gb = (w_in[:, offs[j]:offs[j + 1]] for j in range(len(sizes)))
    glr = jnp.pad(glr, ((0, 0), (0, LANES - GLA_GATE_RANK)))
    w_attn = jnp.concatenate([qa, ka, va], axis=1).astype(_BF16)
    w_gla = jnp.concatenate([qb, kb, vb, rb, ga, gb, glr], axis=1).astype(_BF16)
    return w_attn, w_gla


def kernel(x_prompt, x_sample, cache_a1_k, cache_a1_v, cache_a2_k, cache_a2_v, cache_a3_k, cache_a3_v,
           state_gla, g_norm1, w_in, w_gk2, b_gk, g_gla, w_pa, w_pb, w_o, g_norm2, w_up, w_down, g_final):
    assert g_norm1.shape[0] == 1, "one layer"
    b, t, d = x_prompt.shape
    nb = x_sample.shape[0]
    assert x_sample.shape[1] == 1, "one new token per sample row"

    w_attn, w_gla = _split_w_in(w_in[0])
    w_gk = jnp.pad(w_gk2[0], ((0, LANES - GLA_GATE_RANK), (0, 0))).astype(_BF16)
    b_gk_row = b_gk[0][None, :]
    g1 = g_norm1[0][None, :]
    g2 = g_norm2[0][None, :]
    gf = g_final[None, :]
    gn = g_gla[0][None, :]
    wpa = w_pa[0].astype(_BF16)
    wpb = w_pb[0].astype(_BF16)
    wo = w_o[0].astype(_BF16)
    wup = w_up[0].astype(_BF16)
    wdn = w_down[0].astype(_BF16)

    cos_p, sin_p = _rope_tables(t, 0, 1)
    outs = _attn_proj_prompt(x_prompt, g1, w_attn, cos_p, sin_p)
    packs, p_caches = outs[:3], outs[3:]
    xp = x_prompt.reshape(b * t, d)
    qb, kb, vb, rb, ga, gb, gk = _gla_proj(xp, g1, w_gla, w_gk, b_gk_row, ROW_TILE, _BF16)
    oa = _dilated_attn(packs, t)

    def seq(z):
        return z.reshape(b, t, z.shape[-1])

    ob, p_gla = _gla_chunked(seq(qb), seq(kb), seq(vb), seq(gk), seq(rb), gn)
    hp = _merge_proj(xp, oa.reshape(b * t, A_GROUP_WIDTH), ob.reshape(b * t, GLA_DV), ga, gb, wpa, wpb, wo, ROW_TILE)
    y_prompt = _ffn_final(hp, g2, wup, wdn, gf, ROW_TILE).reshape(b, t, d)

    cos_s, sin_s = _rope_tables(nb, PAST_LEN, 0)
    qkv_s = _attn_proj_sample(x_sample.reshape(1, nb, d), g1, w_attn, cos_s, sin_s).reshape(nb, 1, 3 * A_WIDTH)
    xs = x_sample.reshape(nb, d)
    qb_s, kb_s, vb_s, rb_s, ga_s, gb_s, gk_s = _gla_proj(xs, g1, w_gla, w_gk, b_gk_row, nb, _F32)
    caches = [c[0].reshape(nb, c.shape[2], A_GROUP_WIDTH)
              for c in (cache_a1_k, cache_a1_v, cache_a2_k, cache_a2_v, cache_a3_k, cache_a3_v)]
    s_outs = _sample_attn(qkv_s, caches)
    oa_s, s_caches = s_outs[0], s_outs[1:]

    def tok(z):
        return z.reshape(nb, 1, z.shape[-1])

    ob_s, s_gla = _sample_gla(tok(qb_s), tok(kb_s), tok(vb_s), tok(gk_s), tok(rb_s), gn, state_gla[0])
    hs = _merge_proj(xs, oa_s.reshape(nb, A_GROUP_WIDTH), ob_s.reshape(nb, GLA_DV), ga_s, gb_s, wpa, wpb, wo, nb)
    y_sample = _ffn_final(hs, g2, wup, wdn, gf, nb).reshape(nb, 1, d)

    def cache_out(z):
        return z.reshape(1, z.shape[0], z.shape[1], A_HEADS_PER_GROUP, A_HEAD_DIM)

    return (y_prompt, y_sample, *[cache_out(z) for z in p_caches], p_gla[None],
            *[cache_out(z) for z in s_caches], s_gla[None])
```

```python
import functools

import jax
import jax.numpy as jnp
import numpy as np
from jax import lax
from jax.experimental import pallas as pl
from jax.experimental.pallas import tpu as pltpu

D_MODEL = 1024
PAST_LEN = 16384
A_GROUPS = ((128, 1), (512, 4), (2048, 16))
A_N_GROUPS = 3
A_HEADS_PER_GROUP = 4
A_HEAD_DIM = 64
A_GROUP_WIDTH = A_HEADS_PER_GROUP * A_HEAD_DIM
A_WIDTH = A_N_GROUPS * A_GROUP_WIDTH
A_WINDOW_KEYS = 128
ROPE_THETA = 10000.0
GLA_HEADS = 4
GLA_HEAD_DK = 128
GLA_HEAD_DV = 256
GLA_DK = GLA_HEADS * GLA_HEAD_DK
GLA_DV = GLA_HEADS * GLA_HEAD_DV
GLA_GATE_RANK = 16
GLA_GATE_NORM = 16.0
D_FF = 4 * D_MODEL
EPS = 1e-6
LOG2_E = 1.4426950408889634

LANES = 128
V7X_SCOPED_VMEM_CAP = 56 * 1024 * 1024

ROW_TILE = 512
Q_BLOCK = 128
ATTN_BLOCKS_PER_STEP = 8
GLA_CHUNK = 128
GLA_ROW_TILE = 512
GLA_FACTOR_RANGE = 40.0
FF_CHUNK = 1024

_F32 = jnp.float32
_BF16 = jnp.bfloat16
_NEG_INF = float("-inf")


def _vmem_limit(nbytes):
    return int(min(V7X_SCOPED_VMEM_CAP, max(16 * 1024 * 1024, nbytes * 3 // 2)))


def _resident(shape):
    nd = len(shape)
    return pl.BlockSpec(shape, lambda *_: (0,) * nd, pipeline_mode=pl.Buffered(1))


def _rms_normed(x, g):
    return x * lax.rsqrt(jnp.mean(x * x, axis=-1, keepdims=True) + EPS) * g


def _rope_table_kernel(inv_ref, sign_ref, cos_ref, sin_ref, *, pos0, pos_step):
    rows = cos_ref.shape[0]
    row = lax.broadcasted_iota(jnp.int32, (rows, LANES), 0) + pl.program_id(0) * rows
    pos = (pos0 + row * pos_step).astype(_F32)
    ang = pos * inv_ref[...]
    cos_ref[...] = jnp.cos(ang)
    sin_ref[...] = jnp.sin(ang) * sign_ref[...]


def _rope_tables(n_rows, pos0, pos_step):
    half = A_HEAD_DIM // 2
    lane = jnp.arange(LANES)
    inv = ROPE_THETA ** (-((lane % half).astype(_F32)) / half)
    sign = jnp.where((lane % A_HEAD_DIM) < half, -1.0, 1.0).astype(_F32)
    tile = min(n_rows, ROW_TILE)
    return pl.pallas_call(
        functools.partial(_rope_table_kernel, pos0=pos0, pos_step=pos_step),
        grid=(n_rows // tile,),
        in_specs=[pl.BlockSpec((1, LANES), lambda i: (0, 0))] * 2,
        out_specs=[pl.BlockSpec((tile, LANES), lambda i: (i, 0))] * 2,
        out_shape=[jax.ShapeDtypeStruct((n_rows, LANES), _F32)] * 2,
        name="rope_table",
    )(inv[None, :], sign[None, :])


def _rope_slab(z, cos, sin_signed, first_half):
    half = A_HEAD_DIM // 2
    partner = jnp.where(first_half, pltpu.roll(z, LANES - half, 1), pltpu.roll(z, half, 1))
    return z * cos + partner * sin_signed


_N_SLABS = 3 * A_WIDTH // LANES
_SLABS_PER_SECTION = A_WIDTH // LANES
_SLABS_PER_GROUP = A_GROUP_WIDTH // LANES


def _attn_proj_slabs(x_ref, g_ref, w_ref, cos_ref, sin_ref, q_scale):
    u = _rms_normed(x_ref[0], g_ref[...]).astype(_BF16)
    y = jnp.dot(u, w_ref[...], preferred_element_type=_F32)
    cos = cos_ref[...]
    sin = sin_ref[...]
    lane = lax.broadcasted_iota(jnp.int32, cos.shape, 1)
    first_half = (lane % A_HEAD_DIM) < (A_HEAD_DIM // 2)
    for s in range(_N_SLABS):
        z = y[:, s * LANES:(s + 1) * LANES]
        if s < _SLABS_PER_SECTION:
            z = _rope_slab(z, cos, sin, first_half) * q_scale
        elif s < 2 * _SLABS_PER_SECTION:
            z = _rope_slab(z, cos, sin, first_half)
        yield s, z


def _attn_proj_prompt_kernel(x_ref, g_ref, w_ref, cos_ref, sin_ref,
                             p1_ref, p2_ref, p3_ref, c1k_ref, c1v_ref, c2k_ref, c2v_ref, c3k_ref, c3v_ref,
                             slab_ref, *, tm, n_tiles):
    i = pl.program_id(1)
    packs = (p1_ref, p2_ref, p3_ref)
    caches = ((c1k_ref, c1v_ref), (c2k_ref, c2v_ref), (c3k_ref, c3v_ref))
    for s, z in _attn_proj_slabs(x_ref, g_ref, w_ref, cos_ref, sin_ref, A_HEAD_DIM ** -0.5 * LOG2_E):
        slab_ref[s] = z
    for s in range(_N_SLABS):
        section, within = divmod(s, _SLABS_PER_SECTION)
        group, pair = divmod(within, _SLABS_PER_GROUP)
        dil = A_GROUPS[group][1]
        col = (section * _SLABS_PER_GROUP + pair) * LANES
        for r in range(dil):
            rows = slab_ref[s, pl.ds(r, tm // dil, stride=dil), :] if dil > 1 else slab_ref[s]
            packs[group][0, r, :, col:col + LANES] = rows.astype(_BF16)

    for group, (window, _) in enumerate(A_GROUPS):
        keep = min(window, tm)
        first_kept_tile = n_tiles - max(window // tm, 1)

        @pl.when(i >= first_kept_tile)
        def _(group=group, keep=keep):
            for section in (1, 2):
                for pair in range(_SLABS_PER_GROUP):
                    s = section * _SLABS_PER_SECTION + group * _SLABS_PER_GROUP + pair
                    caches[group][section - 1][0, :, pair * LANES:(pair + 1) * LANES] = slab_ref[s, tm - keep:tm, :]


def _attn_proj_sample_kernel(x_ref, g_ref, w_ref, cos_ref, sin_ref, o_ref):
    for s, z in _attn_proj_slabs(x_ref, g_ref, w_ref, cos_ref, sin_ref, A_HEAD_DIM ** -0.5):
        o_ref[0, :, s * LANES:(s + 1) * LANES] = z


def _attn_proj_prompt(x, g, w, cos, sin):
    b, t, d = x.shape
    tm = ROW_TILE
    n_tiles = t // tm
    pack_shapes, pack_specs = [], []
    for _, dil in A_GROUPS:
        pack_shapes.append(jax.ShapeDtypeStruct((b, dil, t // dil, 3 * A_GROUP_WIDTH), _BF16))
        pack_specs.append(pl.BlockSpec((1, dil, tm // dil, 3 * A_GROUP_WIDTH), lambda bi, i: (bi, 0, i, 0)))
    cache_shapes, cache_specs = [], []
    for window, _ in A_GROUPS:
        keep = min(window, t)
        blk = min(keep, tm)
        first = n_tiles - max(keep // tm, 1)
        for _ in range(2):
            cache_shapes.append(jax.ShapeDtypeStruct((b, keep, A_GROUP_WIDTH), _F32))
            cache_specs.append(pl.BlockSpec(
                (1, blk, A_GROUP_WIDTH), lambda bi, i, first=first: (bi, jnp.maximum(i - first, 0), 0)))
    est = (w.size * 2 + 2 * tm * d * 4 + 2 * 3 * tm * 3 * A_GROUP_WIDTH * 2 + 2 * 6 * tm * A_GROUP_WIDTH * 4
           + 2 * tm * 3 * A_WIDTH * 4)
    return pl.pallas_call(
        functools.partial(_attn_proj_prompt_kernel, tm=tm, n_tiles=n_tiles),
        grid=(b, n_tiles),
        in_specs=[
            pl.BlockSpec((1, tm, d), lambda bi, i: (bi, i, 0)),
            _resident((1, d)),
            _resident(w.shape),
            pl.BlockSpec((tm, LANES), lambda bi, i: (i, 0)),
            pl.BlockSpec((tm, LANES), lambda bi, i: (i, 0)),
        ],
        out_specs=pack_specs + cache_specs,
        out_shape=pack_shapes + cache_shapes,
        scratch_shapes=[pltpu.VMEM((_N_SLABS, tm, LANES), _F32)],
        compiler_params=pltpu.CompilerParams(
            dimension_semantics=("arbitrary", "arbitrary"), vmem_limit_bytes=_vmem_limit(est)),
        name="attn_proj_prompt",
    )(x, g, w, cos, sin)


def _attn_proj_sample(x, g, w, cos, sin):
    _, m, d = x.shape
    est = w.size * 2 * 2 + 4 * m * 3 * A_WIDTH * 4
    return pl.pallas_call(
        _attn_proj_sample_kernel,
        grid=(1,),
        in_specs=[
            pl.BlockSpec((1, m, d), lambda i: (0, 0, 0)),
            pl.BlockSpec((1, d), lambda i: (0, 0)),
            pl.BlockSpec(w.shape, lambda i: (0, 0)),
            pl.BlockSpec((m, LANES), lambda i: (0, 0)),
            pl.BlockSpec((m, LANES), lambda i: (0, 0)),
        ],
        out_specs=pl.BlockSpec((1, m, 3 * A_WIDTH), lambda i: (0, 0, 0)),
        out_shape=jax.ShapeDtypeStruct((1, m, 3 * A_WIDTH), _F32),
        compiler_params=pltpu.CompilerParams(vmem_limit_bytes=_vmem_limit(est)),
        name="attn_proj_sample",
    )(x, g, w, cos, sin)


_GP_SECTIONS = {}
_off = 0
for _name, _width in (("qb", GLA_DK), ("kb", GLA_DK), ("vb", GLA_DV), ("rb", GLA_DV),
                      ("ga", D_MODEL), ("gb", D_MODEL), ("glr", LANES)):
    _GP_SECTIONS[_name] = (_off, _width)
    _off += _width
_GP_WIDTH = _off


def _gla_proj_kernel(x_ref, g_ref, w_ref, wgk_ref, bgk_ref,
                     qb_ref, kb_ref, vb_ref, rb_ref, ga_ref, gb_ref, gk_ref):
    u = _rms_normed(x_ref[...], g_ref[...]).astype(_BF16)

    def section(name):
        off, width = _GP_SECTIONS[name]
        return jnp.dot(u, w_ref[:, off:off + width], preferred_element_type=_F32)

    qb_ref[...] = (section("qb") * (GLA_HEAD_DK ** -0.5)).astype(qb_ref.dtype)
    kb_ref[...] = section("kb").astype(kb_ref.dtype)
    vb_ref[...] = section("vb").astype(vb_ref.dtype)
    rb_ref[...] = section("rb").astype(rb_ref.dtype)
    ga_ref[...] = section("ga").astype(ga_ref.dtype)
    gb_ref[...] = section("gb").astype(gb_ref.dtype)
    glr = section("glr").astype(_BF16)
    z = jnp.dot(glr, wgk_ref[...], preferred_element_type=_F32) + bgk_ref[...]
    log_sigmoid = jnp.minimum(z, 0.0) - jnp.log1p(jnp.exp(-jnp.abs(z)))
    gk_ref[...] = log_sigmoid / GLA_GATE_NORM


def _gla_proj(x, g, w, wgk, bgk, tm, act_dtype):
    m, d = x.shape
    widths = (GLA_DK, GLA_DK, GLA_DV, GLA_DV, D_MODEL, D_MODEL)
    out_shapes = [jax.ShapeDtypeStruct((m, wd), act_dtype) for wd in widths]
    out_shapes.append(jax.ShapeDtypeStruct((m, GLA_DK), _F32))
    out_specs = [pl.BlockSpec((tm, wd), lambda i: (i, 0)) for wd in widths + (GLA_DK,)]
    est = w.size * 2 + 2 * tm * d * 4 + 2 * tm * (sum(widths) + GLA_DK) * 4 + tm * GLA_DV * 4 * 2
    return pl.pallas_call(
        _gla_proj_kernel,
        grid=(m // tm,),
        in_specs=[
            pl.BlockSpec((tm, d), lambda i: (i, 0)),
            _resident((1, d)),
            _resident(w.shape),
            _resident(wgk.shape),
            _resident(bgk.shape),
        ],
        out_specs=out_specs,
        out_shape=out_shapes,
        compiler_params=pltpu.CompilerParams(
            dimension_semantics=("arbitrary",), vmem_limit_bytes=_vmem_limit(est)),
        name="gla_proj",
    )(x, g, w, wgk, bgk)


def _window_attention(q, k, v, bias, head0):
    qb = Q_BLOCK
    zero = jnp.zeros_like(q)
    q_heads = jnp.concatenate([jnp.where(head0, q, zero), jnp.where(head0, zero, q)], axis=0)
    s = lax.dot_general(q_heads, k, (((1,), (1,)), ((), ())), preferred_element_type=_F32)
    s = s + jnp.concatenate([bias, bias], axis=0)
    m = jnp.max(s, axis=1, keepdims=True)
    p = jnp.exp2(s - m)
    l = jnp.sum(p, axis=1, keepdims=True)
    num = jnp.dot(p.astype(_BF16), v, preferred_element_type=_F32)
    return (jnp.where(head0, m[:qb], m[qb:]), jnp.where(head0, l[:qb], l[qb:]),
            jnp.where(head0, num[:qb], num[qb:]))


def _dilated_attn_kernel(q1_ref, k1_ref, v1_ref, q2_ref, k2_ref, v2_ref, q3_ref, k3_ref, v3_ref,
                         o_ref, m_ref, l_ref, n_ref, bias_ref):
    qb = Q_BLOCK
    lane = lax.broadcasted_iota(jnp.int32, (qb, LANES), 1)
    head0 = lane < A_HEAD_DIM
    row = lax.broadcasted_iota(jnp.int32, (qb, 2 * qb), 0)
    col = lax.broadcasted_iota(jnp.int32, (qb, 2 * qb), 1)
    for j, offset in enumerate((0, qb)):
        diff = row - col + offset
        bias_ref[j] = jnp.where((diff >= 0) & (diff <= A_WINDOW_KEYS), 0.0, _NEG_INF)
    groups = ((q1_ref, k1_ref, v1_ref), (q2_ref, k2_ref, v2_ref), (q3_ref, k3_ref, v3_ref))
    first_group, last_group = A_N_GROUPS - 1, 0
    per_step = ATTN_BLOCKS_PER_STEP

    for group in range(A_N_GROUPS - 1, -1, -1):
        q_ref, k_ref, v_ref = groups[group]
        dil = A_GROUPS[group][1]
        n_blocks = q_ref.shape[2] // qb
        res_per_step = min(dil, per_step)
        blk_per_step = per_step // res_per_step
        blk_steps = n_blocks // blk_per_step

        def step(idx, carry, q_ref=q_ref, k_ref=k_ref, v_ref=v_ref, dil=dil, group=group,
                 res_per_step=res_per_step, blk_per_step=blk_per_step, blk_steps=blk_steps):
            stats, rows = [], []
            for u in range(per_step):
                if dil == 1:
                    r = 0
                else:
                    r = (idx // blk_steps) * res_per_step + u % res_per_step
                n = (idx % blk_steps) * blk_per_step + u // res_per_step
                q0 = pl.multiple_of(n * qb, qb)
                k0 = pl.multiple_of(jnp.maximum(n - 1, 0) * qb, qb)
                q = q_ref[0, r, pl.ds(q0, qb), :]
                k = k_ref[0, r, pl.ds(k0, 2 * qb), :]
                v = v_ref[0, r, pl.ds(k0, 2 * qb), :]
                stats.append(_window_attention(q, k, v, bias_ref[jnp.minimum(n, 1)], head0))
                rows.append(pl.ds(q0, qb) if dil == 1 else pl.ds(q0 * dil + r, qb, stride=dil))
            if group == first_group:
                for (m_g, l_g, n_g), rw in zip(stats, rows):
                    m_ref[rw, :] = m_g
                    l_ref[rw, :] = l_g
                    n_ref[rw, :] = n_g
                return carry
            old = [(m_ref[rw, :], l_ref[rw, :], n_ref[rw, :]) for rw in rows]
            for (m_g, l_g, n_g), (m_old, l_old, n_old), rw in zip(stats, old, rows):
                m_new = jnp.maximum(m_old, m_g)
                a = jnp.exp2(m_old - m_new)
                b = jnp.exp2(m_g - m_new)
                l_new = a * l_old + b * l_g
                n_new = a * n_old + b * n_g
                if group == last_group:
                    o_ref[0, rw, :] = (n_new / l_new).astype(o_ref.dtype)
                else:
                    m_ref[rw, :] = m_new
                    l_ref[rw, :] = l_new
                    n_ref[rw, :] = n_new
            return carry

        lax.fori_loop(0, dil * n_blocks // per_step, step, 0)


def _dilated_attn(packs, t):
    b = packs[0].shape[0]
    n_pairs = A_GROUP_WIDTH // LANES
    in_specs, operands = [], []
    for pack, (_, dil) in zip(packs, A_GROUPS):
        for section in range(3):
            in_specs.append(pl.BlockSpec(
                (1, dil, t // dil, LANES), lambda bi, p, section=section: (bi, 0, 0, section * n_pairs + p)))
            operands.append(pack)
    est = 2 * 9 * t * LANES * 2 + 2 * t * LANES * 2 + 3 * t * LANES * 4
    return pl.pallas_call(
        _dilated_attn_kernel,
        grid=(b, n_pairs),
        in_specs=in_specs,
        out_specs=pl.BlockSpec((1, t, LANES), lambda bi, p: (bi, 0, p)),
        out_shape=jax.ShapeDtypeStruct((b, t, A_GROUP_WIDTH), _BF16),
        scratch_shapes=[pltpu.VMEM((t, LANES), _F32)] * 3 + [pltpu.VMEM((2, Q_BLOCK, 2 * Q_BLOCK), _F32)],
        compiler_params=pltpu.CompilerParams(
            dimension_semantics=("arbitrary", "arbitrary"), vmem_limit_bytes=_vmem_limit(est)),
        name="dilated_attn",
    )(*operands)


def _pairwise_decay_scores(q, k, cum):
    c = q.shape[0]
    row = lax.broadcasted_iota(jnp.int32, (c, c), 0)
    col = lax.broadcasted_iota(jnp.int32, (c, c), 1)
    row_d = lax.broadcasted_iota(jnp.int32, q.shape, 0)

    def sub_diagonal(delta, acc):
        k_back = pltpu.roll(k, delta, 0)
        cum_back = pltpu.roll(cum, delta, 0)
        log_decay = jnp.where(row_d >= delta, cum - cum_back, 0.0)
        diag = jnp.sum(q * k_back * jnp.exp(log_decay), axis=1, keepdims=True)
        return acc + jnp.where(row - col == delta, diag, 0.0)

    return lax.fori_loop(0, c, sub_diagonal, jnp.zeros((c, c), _F32))


def _gla_chunked_kernel(q_ref, k_ref, v_ref, gk_ref, r_ref, gn_ref, o_ref, fin_ref, st_ref, cum_ref, sc_ref):
    c = GLA_CHUNK
    n_chunks = q_ref.shape[1] // c
    tile = pl.program_id(1)

    @pl.when(tile == 0)
    def _():
        st_ref[...] = jnp.zeros_like(st_ref)

    row = lax.broadcasted_iota(jnp.int32, (c, c), 0)
    col = lax.broadcasted_iota(jnp.int32, (c, c), 1)
    causal = row >= col
    causal_bf = jnp.where(causal, 1.0, 0.0).astype(_BF16)
    gn = gn_ref[...]

    def head_cols(h):
        return (slice(h * GLA_HEAD_DK, (h + 1) * GLA_HEAD_DK), slice(h * GLA_HEAD_DV, (h + 1) * GLA_HEAD_DV))

    spread = jnp.zeros((1, GLA_DK), _F32)
    for ci in range(n_chunks):
        rows = slice(ci * c, (ci + 1) * c)
        g = gk_ref[0, rows, :]
        g_hi = g.astype(_BF16)
        g_lo = (g - g_hi.astype(_F32)).astype(_BF16)
        parts = jnp.dot(causal_bf, jnp.concatenate([g_hi, g_lo], axis=1), preferred_element_type=_F32)
        cum = parts[:, :GLA_DK] + parts[:, GLA_DK:]
        cum_ref[rows, :] = cum
        mid = cum[c // 2 - 1:c // 2, :]
        spread = jnp.maximum(spread, jnp.maximum(cum[0:1, :] - mid, mid - cum[c - 1:c, :]))
    factorable = jnp.max(spread) <= GLA_FACTOR_RANGE

    @pl.when(factorable)
    def _():
        for ci in range(n_chunks):
            rows = slice(ci * c, (ci + 1) * c)
            for h in range(GLA_HEADS):
                dk, _ = head_cols(h)
                cum = cum_ref[rows, dk]
                mid = cum[c // 2 - 1:c // 2, :]
                q_mid = (q_ref[0, rows, dk].astype(_F32) * jnp.exp(cum - mid)).astype(_BF16)
                k_mid = (k_ref[0, rows, dk].astype(_F32) * jnp.exp(mid - cum)).astype(_BF16)
                scores = lax.dot_general(q_mid, k_mid, (((1,), (1,)), ((), ())), preferred_element_type=_F32)
                sc_ref[ci * GLA_HEADS + h] = jnp.where(causal, scores, 0.0).astype(_BF16)

    @pl.when(jnp.logical_not(factorable))
    def _():
        for ci in range(n_chunks):
            rows = slice(ci * c, (ci + 1) * c)
            for h in range(GLA_HEADS):
                dk, _ = head_cols(h)
                scores = _pairwise_decay_scores(q_ref[0, rows, dk].astype(_F32), k_ref[0, rows, dk].astype(_F32),
                                                cum_ref[rows, dk])
                sc_ref[ci * GLA_HEADS + h] = scores.astype(_BF16)

    states = [st_ref[h] for h in range(GLA_HEADS)]
    for ci in range(n_chunks):
        rows = slice(ci * c, (ci + 1) * c)
        for h in range(GLA_HEADS):
            dk, dv = head_cols(h)
            cum = cum_ref[rows, dk]
            last = cum[c - 1:c, :]
            v = v_ref[0, rows, dv]
            q_in = (q_ref[0, rows, dk].astype(_F32) * jnp.exp(cum)).astype(_BF16)
            k_out = (k_ref[0, rows, dk].astype(_F32) * jnp.exp(last - cum)).astype(_BF16)
            o = jnp.dot(sc_ref[ci * GLA_HEADS + h], v, preferred_element_type=_F32)
            o = o + lax.dot_general(q_in, states[h].astype(_BF16), (((1,), (1,)), ((), ())),
                                    preferred_element_type=_F32)
            update_t = lax.dot_general(v, k_out, (((0,), (0,)), ((), ())), preferred_element_type=_F32)
            states[h] = states[h] * jnp.exp(last) + update_t
            rb = r_ref[0, rows, dv].astype(_F32)
            o = _rms_normed(o, gn) * (rb * jax.nn.sigmoid(rb))
            o_ref[0, rows, dv] = o.astype(o_ref.dtype)
    for h in range(GLA_HEADS):
        st_ref[h] = states[h]

    @pl.when(tile == pl.num_programs(1) - 1)
    def _():
        for h in range(GLA_HEADS):
            fin_ref[0, h] = st_ref[h].T


def _gla_chunked(qb, kb, vb, gk, rb, gn):
    b, t, _ = qb.shape
    tm = GLA_ROW_TILE
    est = 2 * tm * (2 * GLA_DK * 2 + 2 * GLA_DV * 2 + GLA_DK * 4 + GLA_DV * 2) + 3 * GLA_HEADS * GLA_HEAD_DK * GLA_HEAD_DV * 4
    return pl.pallas_call(
        _gla_chunked_kernel,
        grid=(b, t // tm),
        in_specs=[
            pl.BlockSpec((1, tm, GLA_DK), lambda bi, i: (bi, i, 0)),
            pl.BlockSpec((1, tm, GLA_DK), lambda bi, i: (bi, i, 0)),
            pl.BlockSpec((1, tm, GLA_DV), lambda bi, i: (bi, i, 0)),
            pl.BlockSpec((1, tm, GLA_DK), lambda bi, i: (bi, i, 0)),
            pl.BlockSpec((1, tm, GLA_DV), lambda bi, i: (bi, i, 0)),
            pl.BlockSpec((1, GLA_HEAD_DV), lambda bi, i: (0, 0)),
        ],
        out_specs=[
            pl.BlockSpec((1, tm, GLA_DV), lambda bi, i: (bi, i, 0)),
            pl.BlockSpec((1, GLA_HEADS, GLA_HEAD_DK, GLA_HEAD_DV), lambda bi, i: (bi, 0, 0, 0)),
        ],
        out_shape=[
            jax.ShapeDtypeStruct((b, t, GLA_DV), _BF16),
            jax.ShapeDtypeStruct((b, GLA_HEADS, GLA_HEAD_DK, GLA_HEAD_DV), _F32),
        ],
        scratch_shapes=[
            pltpu.VMEM((GLA_HEADS, GLA_HEAD_DV, GLA_HEAD_DK), _F32),
            pltpu.VMEM((tm, GLA_DK), _F32),
            pltpu.VMEM((tm // GLA_CHUNK * GLA_HEADS, GLA_CHUNK, GLA_CHUNK), _BF16),
        ],
        compiler_params=pltpu.CompilerParams(
            dimension_semantics=("arbitrary", "arbitrary"), vmem_limit_bytes=_vmem_limit(est)),
        name="gla_chunked",
    )(qb, kb, vb, gk, rb, gn)


def _sample_attn_kernel(qkv_ref, c1k_ref, c1v_ref, c2k_ref, c2v_ref, c3k_ref, c3v_ref,
                        o_ref, n1k_ref, n1v_ref, n2k_ref, n2v_ref, n3k_ref, n3v_ref):
    caches = ((c1k_ref, c1v_ref, n1k_ref, n1v_ref), (c2k_ref, c2v_ref, n2k_ref, n2v_ref),
              (c3k_ref, c3v_ref, n3k_ref, n3v_ref))
    n_heads = A_HEADS_PER_GROUP
    head_rows = 8
    hrow = lax.broadcasted_iota(jnp.int32, (head_rows, A_GROUP_WIDTH), 0)
    hlane = lax.broadcasted_iota(jnp.int32, (head_rows, A_GROUP_WIDTH), 1)
    own = (hlane // A_HEAD_DIM) == hrow
    stats = []
    for group, (kc_ref, vc_ref, kn_ref, vn_ref) in enumerate(caches):
        window, dil = A_GROUPS[group]
        base = group * A_GROUP_WIDTH
        q = qkv_ref[0, :, base:base + A_GROUP_WIDTH]
        k_new = qkv_ref[0, :, A_WIDTH + base:A_WIDTH + base + A_GROUP_WIDTH]
        v_new = qkv_ref[0, :, 2 * A_WIDTH + base:2 * A_WIDTH + base + A_GROUP_WIDTH]
        kc = kc_ref[0]
        vc = vc_ref[0]
        crow = lax.broadcasted_iota(jnp.int32, (window, A_GROUP_WIDTH), 0)
        is_last = crow == window - 1
        kn_ref[0] = jnp.where(is_last, k_new, pltpu.roll(kc, window - 1, 0))
        vn_ref[0] = jnp.where(is_last, v_new, pltpu.roll(vc, window - 1, 0))

        q_heads = jnp.where(own, q, 0.0)
        s_old = lax.dot_general(q_heads.astype(_BF16), kc.astype(_BF16), (((1,), (1,)), ((), ())),
                                preferred_element_type=_F32)
        pos = lax.broadcasted_iota(jnp.int32, (head_rows, window), 1)
        s_old = jnp.where(pos % dil == 0, s_old, _NEG_INF)
        s_new = jnp.sum(q_heads * k_new, axis=1, keepdims=True)
        m = jnp.maximum(jnp.max(s_old, axis=1, keepdims=True), s_new)
        p_old = jnp.exp(s_old - m)
        p_new = jnp.exp(s_new - m)
        l = jnp.sum(p_old, axis=1, keepdims=True) + p_new
        num = jnp.dot(p_old.astype(_BF16), vc.astype(_BF16), preferred_element_type=_F32) + p_new * v_new
        stats.append((m, l, num))
    m_all = jnp.maximum(jnp.maximum(stats[0][0], stats[1][0]), stats[2][0])
    num = sum(jnp.exp(m - m_all) * n for m, _, n in stats)
    den = sum(jnp.exp(m - m_all) * l for m, l, _ in stats)
    o = jnp.where(own, num / den, 0.0)
    del n_heads
    o_ref[0] = jnp.sum(o, axis=0, keepdims=True)


def _sample_attn(qkv, caches):
    nb = qkv.shape[0]
    in_specs = [pl.BlockSpec((1, 1, 3 * A_WIDTH), lambda i: (i, 0, 0))]
    out_specs = [pl.BlockSpec((1, 1, A_GROUP_WIDTH), lambda i: (i, 0, 0))]
    out_shapes = [jax.ShapeDtypeStruct((nb, 1, A_GROUP_WIDTH), _F32)]
    est = 0
    for c in caches:
        spec = pl.BlockSpec((1,) + c.shape[1:], lambda i: (i, 0, 0))
        in_specs.append(spec)
        out_specs.append(spec)
        out_shapes.append(jax.ShapeDtypeStruct(c.shape, c.dtype))
        est += 6 * c.shape[1] * c.shape[2] * 4
    return pl.pallas_call(
        _sample_attn_kernel,
        grid=(nb,),
        in_specs=in_specs,
        out_specs=out_specs,
        out_shape=out_shapes,
        compiler_params=pltpu.CompilerParams(
            dimension_semantics=("arbitrary",), vmem_limit_bytes=_vmem_limit(est)),
        name="sample_attn",
    )(qkv, *caches)


def _as_column(row_vec):
    n = row_vec.shape[1]
    eye = lax.broadcasted_iota(jnp.int32, (n, n), 0) == lax.broadcasted_iota(jnp.int32, (n, n), 1)
    return jnp.sum(jnp.where(eye, row_vec, 0.0), axis=1, keepdims=True)


def _sample_gla_kernel(q_ref, k_ref, v_ref, gk_ref, r_ref, gn_ref, st_ref, o_ref, new_ref):
    gn = gn_ref[...]
    for h in range(GLA_HEADS):
        dk = slice(h * GLA_HEAD_DK, (h + 1) * GLA_HEAD_DK)
        dv = slice(h * GLA_HEAD_DV, (h + 1) * GLA_HEAD_DV)
        decay = _as_column(jnp.exp(gk_ref[0, :, dk]))
        k_col = _as_column(k_ref[0, :, dk])
        q_col = _as_column(q_ref[0, :, dk])
        state = decay * st_ref[0, h] + k_col * v_ref[0, :, dv]
        new_ref[0, h] = state
        o = jnp.sum(q_col * state, axis=0, keepdims=True)
        rb = r_ref[0, :, dv]
        o_ref[0, :, dv] = _rms_normed(o, gn) * (rb * jax.nn.sigmoid(rb))


def _sample_gla(qb, kb, vb, gk, rb, gn, state):
    nb = qb.shape[0]

    def row_spec(width):
        return pl.BlockSpec((1, 1, width), lambda i: (i, 0, 0))

    state_spec = pl.BlockSpec((1,) + state.shape[1:], lambda i: (i, 0, 0, 0))
    return pl.pallas_call(
        _sample_gla_kernel,
        grid=(nb,),
        in_specs=[row_spec(GLA_DK), row_spec(GLA_DK), row_spec(GLA_DV), row_spec(GLA_DK), row_spec(GLA_DV),
                  pl.BlockSpec((1, GLA_HEAD_DV), lambda i: (0, 0)), state_spec],
        out_specs=[row_spec(GLA_DV), state_spec],
        out_shape=[jax.ShapeDtypeStruct((nb, 1, GLA_DV), _F32), jax.ShapeDtypeStruct(state.shape, state.dtype)],
        compiler_params=pltpu.CompilerParams(dimension_semantics=("arbitrary",)),
        name="sample_gla",
    )(qb, kb, vb, gk, rb, gn, state)


def _merge_proj_kernel(x_ref, oa_ref, ob_ref, ga_ref, gb_ref, wpa_ref, wpb_ref, wo_ref, h_ref):
    ya = jnp.dot(oa_ref[...].astype(_BF16), wpa_ref[...], preferred_element_type=_F32)
    yb = jnp.dot(ob_ref[...].astype(_BF16), wpb_ref[...], preferred_element_type=_F32)
    mix = jax.nn.sigmoid(ga_ref[...].astype(_F32)) * ya + jax.nn.sigmoid(gb_ref[...].astype(_F32)) * yb
    h_ref[...] = x_ref[...] + jnp.dot(mix.astype(_BF16), wo_ref[...], preferred_element_type=_F32)


def _merge_proj(x, oa, ob, ga, gb, wpa, wpb, wo, tm):
    m, d = x.shape

    def rows(width):
        return pl.BlockSpec((tm, width), lambda i: (i, 0))

    est = (wpa.size + wpb.size + wo.size) * 2 + 2 * tm * (2 * d * 4 + A_GROUP_WIDTH * 4 + 3 * d * 4) + 4 * tm * d * 4
    return pl.pallas_call(
        _merge_proj_kernel,
        grid=(m // tm,),
        in_specs=[rows(d), rows(A_GROUP_WIDTH), rows(GLA_DV), rows(d), rows(d),
                  _resident(wpa.shape), _resident(wpb.shape), _resident(wo.shape)],
        out_specs=rows(d),
        out_shape=jax.ShapeDtypeStruct((m, d), _F32),
        compiler_params=pltpu.CompilerParams(
            dimension_semantics=("arbitrary",), vmem_limit_bytes=_vmem_limit(est)),
        name="merge_proj",
    )(x, oa, ob, ga, gb, wpa, wpb, wo)


def _ffn_final_kernel(h_ref, g2_ref, wup_ref, wdn_ref, gf_ref, y_ref):
    h = h_ref[...]
    hn = _rms_normed(h, g2_ref[...]).astype(_BF16)
    acc = h
    for c0 in range(0, D_FF, FF_CHUNK):
        f = jnp.maximum(jnp.dot(hn, wup_ref[:, c0:c0 + FF_CHUNK], preferred_element_type=_F32), 0.0)
        acc = acc + jnp.dot((f * f).astype(_BF16), wdn_ref[c0:c0 + FF_CHUNK, :], preferred_element_type=_F32)
    y_ref[...] = _rms_normed(acc, gf_ref[...])


def _ffn_final(h, g2, wup, wdn, gf, tm):
    m, d = h.shape
    est = (wup.size + wdn.size) * 2 + 4 * tm * d * 4 + 3 * tm * FF_CHUNK * 4 + 2 * tm * d * 4
    return pl.pallas_call(
        _ffn_final_kernel,
        grid=(m // tm,),
        in_specs=[pl.BlockSpec((tm, d), lambda i: (i, 0)), _resident((1, d)),
                  _resident(wup.shape), _resident(wdn.shape), _resident((1, d))],
        out_specs=pl.BlockSpec((tm, d), lambda i: (i, 0)),
        out_shape=jax.ShapeDtypeStruct((m, d), _F32),
        compiler_params=pltpu.CompilerParams(
            dimension_semantics=("arbitrary",), vmem_limit_bytes=_vmem_limit(est)),
        name="ffn_final",
    )(h, g2, wup, wdn, gf)


def _split_w_in(w_in):
    sizes = (A_WIDTH, A_WIDTH, A_WIDTH, GLA_DK, GLA_DK, GLA_DV, GLA_DV, GLA_GATE_RANK, D_MODEL, D_MODEL)
    offs = np.concatenate([[0], np.cumsum(sizes)])
    qa, ka, va, qb, kb, vb, rb, glr, ga, gb = (w_in[:, offs[j]:offs[j + 1]] for j in range(len(sizes)))
    glr = jnp.pad(glr, ((0, 0), (0, LANES - GLA_GATE_RANK)))
    w_attn = jnp.concatenate([qa, ka, va], axis=1).astype(_BF16)
    w_gla = jnp.concatenate([qb, kb, vb, rb, ga, gb, glr], axis=1).astype(_BF16)
    return w_attn, w_gla


def kernel(x_prompt, x_sample, cache_a1_k, cache_a1_v, cache_a2_k, cache_a2_v, cache_a3_k, cache_a3_v,
           state_gla, g_norm1, w_in, w_gk2, b_gk, g_gla, w_pa, w_pb, w_o, g_norm2, w_up, w_down, g_final):
    assert g_norm1.shape[0] == 1, "one layer"
    b, t, d = x_prompt.shape
    nb = x_sample.shape[0]
    assert x_sample.shape[1] == 1, "one new token per sample row"

    w_attn, w_gla = _split_w_in(w_in[0])
    w_gk = jnp.pad(w_gk2[0], ((0, LANES - GLA_GATE_RANK), (0, 0))).astype(_BF16)
    b_gk_row = b_gk[0][None, :]
    g1 = g_norm1[0][None, :]
    g2 = g_norm2[0][None, :]
    gf = g_final[None, :]
    gn = g_gla[0][None, :]
    wpa = w_pa[0].astype(_BF16)
    wpb = w_pb[0].astype(_BF16)
    wo = w_o[0].astype(_BF16)
    wup = w_up[0].astype(_BF16)
    wdn = w_down[0].astype(_BF16)

    cos_p, sin_p = _rope_tables(t, 0, 1)
    outs = _attn_proj_prompt(x_prompt, g1, w_attn, cos_p, sin_p)
    packs, p_caches = outs[:3], outs[3:]
    xp = x_prompt.reshape(b * t, d)
    qb, kb, vb, rb, ga, gb, gk = _gla_proj(xp, g1, w_gla, w_gk, b_gk_row, ROW_TILE, _BF16)
    oa = _dilated_attn(packs, t)

    def seq(z):
        return z.reshape(b, t, z.shape[-1])

    ob, p_gla = _gla_chunked(seq(qb), seq(kb), seq(vb), seq(gk), seq(rb), gn)
    hp = _merge_proj(xp, oa.reshape(b * t, A_GROUP_WIDTH), ob.reshape(b * t, GLA_DV), ga, gb, wpa, wpb, wo, ROW_TILE)
    y_prompt = _ffn_final(hp, g2, wup, wdn, gf, ROW_TILE).reshape(b, t, d)

    cos_s, sin_s = _rope_tables(nb, PAST_LEN, 0)
    qkv_s = _attn_proj_sample(x_sample.reshape(1, nb, d), g1, w_attn, cos_s, sin_s).reshape(nb, 1, 3 * A_WIDTH)
    xs = x_sample.reshape(nb, d)
    qb_s, kb_s, vb_s, rb_s, ga_s, gb_s, gk_s = _gla_proj(xs, g1, w_gla, w_gk, b_gk_row, nb, _F32)
    caches = [c[0].reshape(nb, c.shape[2], A_GROUP_WIDTH)
              for c in (cache_a1_k, cache_a1_v, cache_a2_k, cache_a2_v, cache_a3_k, cache_a3_v)]
    s_outs = _sample_attn(qkv_s, caches)
    oa_s, s_caches = s_outs[0], s_outs[1:]

    def tok(z):
        return z.reshape(nb, 1, z.shape[-1])

    ob_s, s_gla = _sample_gla(tok(qb_s), tok(kb_s), tok(vb_s), tok(gk_s), tok(rb_s), gn, state_gla[0])
    hs = _merge_proj(xs, oa_s.reshape(nb, A_GROUP_WIDTH), ob_s.reshape(nb, GLA_DV), ga_s, gb_s, wpa, wpb, wo, nb)
    y_sample = _ffn_final(hs, g2, wup, wdn, gf, nb).reshape(nb, 1, d)

    def cache_out(z):
        return z.reshape(1, z.shape[0], z.shape[1], A_HEADS_PER_GROUP, A_HEAD_DIM)

    return (y_prompt, y_sample, *[cache_out(z) for z in p_caches], p_gla[None],
            *[cache_out(z) for z in s_caches], s_gla[None])
```

```python
import functools

import jax
import jax.numpy as jnp
import numpy as np
from jax import lax
from jax.experimental import pallas as pl
from jax.experimental.pallas import tpu as pltpu

D_MODEL = 1024
PAST_LEN = 16384
A_GROUPS = ((128, 1), (512, 4), (2048, 16))
A_N_GROUPS = 3
A_HEADS_PER_GROUP = 4
A_HEAD_DIM = 64
A_GROUP_WIDTH = A_HEADS_PER_GROUP * A_HEAD_DIM
A_WIDTH = A_N_GROUPS * A_GROUP_WIDTH
A_WINDOW_KEYS = 128
ROPE_THETA = 10000.0
GLA_HEADS = 4
GLA_HEAD_DK = 128
GLA_HEAD_DV = 256
GLA_DK = GLA_HEADS * GLA_HEAD_DK
GLA_DV = GLA_HEADS * GLA_HEAD_DV
GLA_GATE_RANK = 16
GLA_GATE_NORM = 16.0
D_FF = 4 * D_MODEL
EPS = 1e-6
LOG2_E = 1.4426950408889634

LANES = 128
V7X_SCOPED_VMEM_CAP = 56 * 1024 * 1024

ROW_TILE = 512
ROW_SUBTILE = 256
Q_BLOCK = 128
ATTN_BLOCKS_PER_STEP = 8
GLA_CHUNK = 128
GLA_ROW_TILE = 512
GLA_FACTOR_RANGE = 40.0
FF_CHUNK = 1024

_F32 = jnp.float32
_BF16 = jnp.bfloat16
_NEG_INF = float("-inf")


def _vmem_limit(nbytes):
    return int(min(V7X_SCOPED_VMEM_CAP, max(16 * 1024 * 1024, nbytes * 3 // 2)))


def _resident(shape):
    nd = len(shape)
    return pl.BlockSpec(shape, lambda *_: (0,) * nd, pipeline_mode=pl.Buffered(1))


def _rms_normed(x, g):
    return x * lax.rsqrt(jnp.mean(x * x, axis=-1, keepdims=True) + EPS) * g


def _rope_table_kernel(inv_ref, sign_ref, cos_ref, sin_ref, *, pos0, pos_step):
    rows = cos_ref.shape[0]
    row = lax.broadcasted_iota(jnp.int32, (rows, LANES), 0) + pl.program_id(0) * rows
    pos = (pos0 + row * pos_step).astype(_F32)
    ang = pos * inv_ref[...]
    cos_ref[...] = jnp.cos(ang)
    sin_ref[...] = jnp.sin(ang) * sign_ref[...]


def _rope_tables(n_rows, pos0, pos_step):
    half = A_HEAD_DIM // 2
    lane = jnp.arange(LANES)
    inv = ROPE_THETA ** (-((lane % half).astype(_F32)) / half)
    sign = jnp.where((lane % A_HEAD_DIM) < half, -1.0, 1.0).astype(_F32)
    tile = min(n_rows, ROW_TILE)
    return pl.pallas_call(
        functools.partial(_rope_table_kernel, pos0=pos0, pos_step=pos_step),
        grid=(n_rows // tile,),
        in_specs=[pl.BlockSpec((1, LANES), lambda i: (0, 0))] * 2,
        out_specs=[pl.BlockSpec((tile, LANES), lambda i: (i, 0))] * 2,
        out_shape=[jax.ShapeDtypeStruct((n_rows, LANES), _F32)] * 2,
        name="rope_table",
    )(inv[None, :], sign[None, :])


def _rope_slab(z, cos, sin_signed, first_half):
    half = A_HEAD_DIM // 2
    partner = jnp.where(first_half, pltpu.roll(z, LANES - half, 1), pltpu.roll(z, half, 1))
    return z * cos + partner * sin_signed


_N_SLABS = 3 * A_WIDTH // LANES
_SLABS_PER_SECTION = A_WIDTH // LANES
_SLABS_PER_GROUP = A_GROUP_WIDTH // LANES


def _attn_proj_slabs(x, g, w_ref, cos, sin, q_scale):
    u = _rms_normed(x, g).astype(_BF16)
    y = jnp.dot(u, w_ref[...], preferred_element_type=_F32)
    lane = lax.broadcasted_iota(jnp.int32, cos.shape, 1)
    first_half = (lane % A_HEAD_DIM) < (A_HEAD_DIM // 2)
    for s in range(_N_SLABS):
        z = y[:, s * LANES:(s + 1) * LANES]
        if s < _SLABS_PER_SECTION:
            z = _rope_slab(z, cos, sin, first_half) * q_scale
        elif s < 2 * _SLABS_PER_SECTION:
            z = _rope_slab(z, cos, sin, first_half)
        yield s, z


def _attn_proj_prompt_kernel(x_ref, g_ref, w_ref, cos_ref, sin_ref,
                             p1_ref, p2_ref, p3_ref, c1k_ref, c1v_ref, c2k_ref, c2v_ref, c3k_ref, c3v_ref,
                             slab_ref, *, tm, n_tiles):
    i = pl.program_id(1)
    packs = (p1_ref, p2_ref, p3_ref)
    caches = ((c1k_ref, c1v_ref), (c2k_ref, c2v_ref), (c3k_ref, c3v_ref))
    sub = min(tm, ROW_SUBTILE)
    for r0 in range(0, tm, sub):
        rows = slice(r0, r0 + sub)
        for s, z in _attn_proj_slabs(x_ref[0, rows, :], g_ref[...], w_ref, cos_ref[rows, :], sin_ref[rows, :],
                                     A_HEAD_DIM ** -0.5 * LOG2_E):
            slab_ref[s, rows, :] = z
            section, within = divmod(s, _SLABS_PER_SECTION)
            group, pair = divmod(within, _SLABS_PER_GROUP)
            dil = A_GROUPS[group][1]
            col = (section * _SLABS_PER_GROUP + pair) * LANES
            for r in range(dil):
                picked = slab_ref[s, pl.ds(r0 + r, sub // dil, stride=dil), :] if dil > 1 else z
                packs[group][0, r, r0 // dil:(r0 + sub) // dil, col:col + LANES] = picked.astype(_BF16)

    for group, (window, _) in enumerate(A_GROUPS):
        keep = min(window, tm)
        first_kept_tile = n_tiles - max(window // tm, 1)

        @pl.when(i >= first_kept_tile)
        def _(group=group, keep=keep):
            for section in (1, 2):
                for pair in range(_SLABS_PER_GROUP):
                    s = section * _SLABS_PER_SECTION + group * _SLABS_PER_GROUP + pair
                    caches[group][section - 1][0, pair * LANES:(pair + 1) * LANES, :] = slab_ref[s, tm - keep:tm, :].T


def _attn_proj_sample_kernel(x_ref, g_ref, w_ref, cos_ref, sin_ref, o_ref):
    for s, z in _attn_proj_slabs(x_ref[0], g_ref[...], w_ref, cos_ref[...], sin_ref[...], A_HEAD_DIM ** -0.5):
        o_ref[0, :, s * LANES:(s + 1) * LANES] = z


def _attn_proj_prompt(x, g, w, cos, sin):
    b, t, d = x.shape
    tm = ROW_TILE
    n_tiles = t // tm
    pack_shapes, pack_specs = [], []
    for _, dil in A_GROUPS:
        pack_shapes.append(jax.ShapeDtypeStruct((b, dil, t // dil, 3 * A_GROUP_WIDTH), _BF16))
        pack_specs.append(pl.BlockSpec((1, dil, tm // dil, 3 * A_GROUP_WIDTH), lambda bi, i: (bi, 0, i, 0)))
    cache_shapes, cache_specs = [], []
    for window, _ in A_GROUPS:
        keep = min(window, t)
        blk = min(keep, tm)
        first = n_tiles - max(keep // tm, 1)
        for _ in range(2):
            cache_shapes.append(jax.ShapeDtypeStruct((b, A_GROUP_WIDTH, keep), _F32))
            cache_specs.append(pl.BlockSpec(
                (1, A_GROUP_WIDTH, blk), lambda bi, i, first=first: (bi, 0, jnp.maximum(i - first, 0))))
    est = (w.size * 2 + 2 * tm * d * 4 + 2 * 3 * tm * 3 * A_GROUP_WIDTH * 2 + 2 * 6 * tm * A_GROUP_WIDTH * 4
           + 2 * tm * 3 * A_WIDTH * 4)
    return pl.pallas_call(
        functools.partial(_attn_proj_prompt_kernel, tm=tm, n_tiles=n_tiles),
        grid=(b, n_tiles),
        in_specs=[
            pl.BlockSpec((1, tm, d), lambda bi, i: (bi, i, 0)),
            _resident((1, d)),
            _resident(w.shape),
            pl.BlockSpec((tm, LANES), lambda bi, i: (i, 0)),
            pl.BlockSpec((tm, LANES), lambda bi, i: (i, 0)),
        ],
        out_specs=pack_specs + cache_specs,
        out_shape=pack_shapes + cache_shapes,
        scratch_shapes=[pltpu.VMEM((_N_SLABS, tm, LANES), _F32)],
        compiler_params=pltpu.CompilerParams(
            dimension_semantics=("arbitrary", "arbitrary"), vmem_limit_bytes=_vmem_limit(est)),
        name="attn_proj_prompt",
    )(x, g, w, cos, sin)


def _attn_proj_sample(x, g, w, cos, sin):
    _, m, d = x.shape
    est = w.size * 2 * 2 + 4 * m * 3 * A_WIDTH * 4
    return pl.pallas_call(
        _attn_proj_sample_kernel,
        grid=(1,),
        in_specs=[
            pl.BlockSpec((1, m, d), lambda i: (0, 0, 0)),
            pl.BlockSpec((1, d), lambda i: (0, 0)),
            pl.BlockSpec(w.shape, lambda i: (0, 0)),
            pl.BlockSpec((m, LANES), lambda i: (0, 0)),
            pl.BlockSpec((m, LANES), lambda i: (0, 0)),
        ],
        out_specs=pl.BlockSpec((1, m, 3 * A_WIDTH), lambda i: (0, 0, 0)),
        out_shape=jax.ShapeDtypeStruct((1, m, 3 * A_WIDTH), _F32),
        compiler_params=pltpu.CompilerParams(vmem_limit_bytes=_vmem_limit(est)),
        name="attn_proj_sample",
    )(x, g, w, cos, sin)


_GP_SECTIONS = {}
_off = 0
for _name, _width in (("qb", GLA_DK), ("kb", GLA_DK), ("vb", GLA_DV), ("rb", GLA_DV),
                      ("ga", D_MODEL), ("gb", D_MODEL), ("glr", LANES)):
    _GP_SECTIONS[_name] = (_off, _width)
    _off += _width
_GP_WIDTH = _off


def _gla_proj_kernel(x_ref, g_ref, w_ref, wgk_ref, bgk_ref,
                     qb_ref, kb_ref, vb_ref, rb_ref, ga_ref, gb_ref, gk_ref):
    tm = x_ref.shape[0]
    sub = min(tm, ROW_SUBTILE)
    for r0 in range(0, tm, sub):
        rows = slice(r0, r0 + sub)
        u = _rms_normed(x_ref[rows, :], g_ref[...]).astype(_BF16)

        def section(name, u=u):
            off, width = _GP_SECTIONS[name]
            return jnp.dot(u, w_ref[:, off:off + width], preferred_element_type=_F32)

        glr = section("glr").astype(_BF16)
        z = jnp.dot(glr, wgk_ref[...], preferred_element_type=_F32) + bgk_ref[...]
        log_sigmoid = jnp.minimum(z, 0.0) - jnp.log1p(jnp.exp(-jnp.abs(z)))
        gk_ref[rows, :] = log_sigmoid / GLA_GATE_NORM
        qb_ref[rows, :] = (section("qb") * (GLA_HEAD_DK ** -0.5)).astype(qb_ref.dtype)
        kb_ref[rows, :] = section("kb").astype(kb_ref.dtype)
        vb_ref[rows, :] = section("vb").astype(vb_ref.dtype)
        rb_ref[rows, :] = section("rb").astype(rb_ref.dtype)
        ga_ref[rows, :] = section("ga").astype(ga_ref.dtype)
        gb_ref[rows, :] = section("gb").astype(gb_ref.dtype)


def _gla_proj(x, g, w, wgk, bgk, tm, act_dtype):
    m, d = x.shape
    widths = (GLA_DK, GLA_DK, GLA_DV, GLA_DV, D_MODEL, D_MODEL)
    out_shapes = [jax.ShapeDtypeStruct((m, wd), act_dtype) for wd in widths]
    out_shapes.append(jax.ShapeDtypeStruct((m, GLA_DK), _F32))
    out_specs = [pl.BlockSpec((tm, wd), lambda i: (i, 0)) for wd in widths + (GLA_DK,)]
    est = w.size * 2 + 2 * tm * d * 4 + 2 * tm * (sum(widths) + GLA_DK) * 4 + tm * GLA_DV * 4 * 2
    return pl.pallas_call(
        _gla_proj_kernel,
        grid=(m // tm,),
        in_specs=[
            pl.BlockSpec((tm, d), lambda i: (i, 0)),
            _resident((1, d)),
            _resident(w.shape),
            _resident(wgk.shape),
            _resident(bgk.shape),
        ],
        out_specs=out_specs,
        out_shape=out_shapes,
        compiler_params=pltpu.CompilerParams(
            dimension_semantics=("arbitrary",), vmem_limit_bytes=_vmem_limit(est)),
        name="gla_proj",
    )(x, g, w, wgk, bgk)


def _window_attention(q, k, v, bias, head0):
    qb = Q_BLOCK
    zero = jnp.zeros_like(q)
    q_heads = jnp.concatenate([jnp.where(head0, q, zero), jnp.where(head0, zero, q)], axis=0)
    s = lax.dot_general(q_heads, k, (((1,), (1,)), ((), ())), preferred_element_type=_F32)
    s = s + jnp.concatenate([bias, bias], axis=0)
    m = jnp.max(s, axis=1, keepdims=True)
    p = jnp.exp2(s - m)
    l = jnp.sum(p, axis=1, keepdims=True)
    num = jnp.dot(p.astype(_BF16), v, preferred_element_type=_F32)
    return (jnp.where(head0, m[:qb], m[qb:]), jnp.where(head0, l[:qb], l[qb:]),
            jnp.where(head0, num[:qb], num[qb:]))


def _dilated_attn_kernel(q1_ref, k1_ref, v1_ref, q2_ref, k2_ref, v2_ref, q3_ref, k3_ref, v3_ref,
                         o_ref, m_ref, l_ref, n_ref, bias_ref):
    qb = Q_BLOCK
    lane = lax.broadcasted_iota(jnp.int32, (qb, LANES), 1)
    head0 = lane < A_HEAD_DIM
    row = lax.broadcasted_iota(jnp.int32, (qb, 2 * qb), 0)
    col = lax.broadcasted_iota(jnp.int32, (qb, 2 * qb), 1)
    for j, offset in enumerate((0, qb)):
        diff = row - col + offset
        bias_ref[j] = jnp.where((diff >= 0) & (diff <= A_WINDOW_KEYS), 0.0, _NEG_INF)
    groups = ((q1_ref, k1_ref, v1_ref), (q2_ref, k2_ref, v2_ref), (q3_ref, k3_ref, v3_ref))
    first_group, last_group = A_N_GROUPS - 1, 0
    per_step = ATTN_BLOCKS_PER_STEP

    for group in range(A_N_GROUPS - 1, -1, -1):
        q_ref, k_ref, v_ref = groups[group]
        dil = A_GROUPS[group][1]
        n_blocks = q_ref.shape[2] // qb
        res_per_step = min(dil, per_step)
        blk_per_step = per_step // res_per_step
        blk_steps = n_blocks // blk_per_step

        def step(idx, carry, q_ref=q_ref, k_ref=k_ref, v_ref=v_ref, dil=dil, group=group,
                 res_per_step=res_per_step, blk_per_step=blk_per_step, blk_steps=blk_steps):
            stats, rows = [], []
            for u in range(per_step):
                if dil == 1:
                    r = 0
                else:
                    r = (idx // blk_steps) * res_per_step + u % res_per_step
                n = (idx % blk_steps) * blk_per_step + u // res_per_step
                q0 = pl.multiple_of(n * qb, qb)
                k0 = pl.multiple_of(jnp.maximum(n - 1, 0) * qb, qb)
                q = q_ref[0, r, pl.ds(q0, qb), :]
                k = k_ref[0, r, pl.ds(k0, 2 * qb), :]
                v = v_ref[0, r, pl.ds(k0, 2 * qb), :]
                stats.append(_window_attention(q, k, v, bias_ref[jnp.minimum(n, 1)], head0))
                rows.append(pl.ds(q0, qb) if dil == 1 else pl.ds(q0 * dil + r, qb, stride=dil))
            if group == first_group:
                for (m_g, l_g, n_g), rw in zip(stats, rows):
                    m_ref[rw, :] = m_g
                    l_ref[rw, :] = l_g
                    n_ref[rw, :] = n_g
                return carry
            old = [(m_ref[rw, :], l_ref[rw, :], n_ref[rw, :]) for rw in rows]
            for (m_g, l_g, n_g), (m_old, l_old, n_old), rw in zip(stats, old, rows):
                m_new = jnp.maximum(m_old, m_g)
                a = jnp.exp2(m_old - m_new)
                b = jnp.exp2(m_g - m_new)
                l_new = a * l_old + b * l_g
                n_new = a * n_old + b * n_g
                if group == last_group:
                    o_ref[0, rw, :] = (n_new / l_new).astype(o_ref.dtype)
                else:
                    m_ref[rw, :] = m_new
                    l_ref[rw, :] = l_new
                    n_ref[rw, :] = n_new
            return carry

        lax.fori_loop(0, dil * n_blocks // per_step, step, 0)


def _dilated_attn(packs, t):
    b = packs[0].shape[0]
    n_pairs = A_GROUP_WIDTH // LANES
    in_specs, operands = [], []
    for pack, (_, dil) in zip(packs, A_GROUPS):
        for section in range(3):
            in_specs.append(pl.BlockSpec(
                (1, dil, t // dil, LANES), lambda bi, p, section=section: (bi, 0, 0, section * n_pairs + p)))
            operands.append(pack)
    est = 2 * 9 * t * LANES * 2 + 2 * t * LANES * 2 + 3 * t * LANES * 4
    return pl.pallas_call(
        _dilated_attn_kernel,
        grid=(b, n_pairs),
        in_specs=in_specs,
        out_specs=pl.BlockSpec((1, t, LANES), lambda bi, p: (bi, 0, p)),
        out_shape=jax.ShapeDtypeStruct((b, t, A_GROUP_WIDTH), _BF16),
        scratch_shapes=[pltpu.VMEM((t, LANES), _F32)] * 3 + [pltpu.VMEM((2, Q_BLOCK, 2 * Q_BLOCK), _F32)],
        compiler_params=pltpu.CompilerParams(
            dimension_semantics=("arbitrary", "arbitrary"), vmem_limit_bytes=_vmem_limit(est)),
        name="dilated_attn",
    )(*operands)


def _pairwise_decay_scores(q, k, cum):
    c = q.shape[0]
    row = lax.broadcasted_iota(jnp.int32, (c, c), 0)
    col = lax.broadcasted_iota(jnp.int32, (c, c), 1)
    row_d = lax.broadcasted_iota(jnp.int32, q.shape, 0)

    def sub_diagonal(delta, acc):
        k_back = pltpu.roll(k, delta, 0)
        cum_back = pltpu.roll(cum, delta, 0)
        log_decay = jnp.where(row_d >= delta, cum - cum_back, 0.0)
        diag = jnp.sum(q * k_back * jnp.exp(log_decay), axis=1, keepdims=True)
        return acc + jnp.where(row - col == delta, diag, 0.0)

    return lax.fori_loop(0, c, sub_diagonal, jnp.zeros((c, c), _F32))


def _gla_chunked_kernel(q_ref, k_ref, v_ref, gk_ref, r_ref, gn_ref, o_ref, fin_ref, st_ref, cum_ref, sc_ref):
    c = GLA_CHUNK
    n_chunks = q_ref.shape[1] // c
    tile = pl.program_id(1)

    @pl.when(tile == 0)
    def _():
        st_ref[...] = jnp.zeros_like(st_ref)

    row = lax.broadcasted_iota(jnp.int32, (c, c), 0)
    col = lax.broadcasted_iota(jnp.int32, (c, c), 1)
    causal = row >= col
    causal_bf = jnp.where(causal, 1.0, 0.0).astype(_BF16)
    gn = gn_ref[...]

    def head_cols(h):
        return (slice(h * GLA_HEAD_DK, (h + 1) * GLA_HEAD_DK), slice(h * GLA_HEAD_DV, (h + 1) * GLA_HEAD_DV))

    spread = jnp.zeros((1, GLA_DK), _F32)
    for ci in range(n_chunks):
        rows = slice(ci * c, (ci + 1) * c)
        g = gk_ref[0, rows, :]
        g_hi = g.astype(_BF16)
        g_lo = (g - g_hi.astype(_F32)).astype(_BF16)
        parts = jnp.dot(causal_bf, jnp.concatenate([g_hi, g_lo], axis=1), preferred_element_type=_F32)
        cum = parts[:, :GLA_DK] + parts[:, GLA_DK:]
        cum_ref[rows, :] = cum
        mid = cum[c // 2 - 1:c // 2, :]
        spread = jnp.maximum(spread, jnp.maximum(cum[0:1, :] - mid, mid - cum[c - 1:c, :]))
    factorable = jnp.max(spread) <= GLA_FACTOR_RANGE

    @pl.when(factorable)
    def _():
        for ci in range(n_chunks):
            rows = slice(ci * c, (ci + 1) * c)
            for h in range(GLA_HEADS):
                dk, _ = head_cols(h)
                cum = cum_ref[rows, dk]
                mid = cum[c // 2 - 1:c // 2, :]
                q_mid = (q_ref[0, rows, dk].astype(_F32) * jnp.exp(cum - mid)).astype(_BF16)
                k_mid = (k_ref[0, rows, dk].astype(_F32) * jnp.exp(mid - cum)).astype(_BF16)
                scores = lax.dot_general(q_mid, k_mid, (((1,), (1,)), ((), ())), preferred_element_type=_F32)
                sc_ref[ci * GLA_HEADS + h] = jnp.where(causal, scores, 0.0).astype(_BF16)

    @pl.when(jnp.logical_not(factorable))
    def _():
        for ci in range(n_chunks):
            rows = slice(ci * c, (ci + 1) * c)
            for h in range(GLA_HEADS):
                dk, _ = head_cols(h)
                scores = _pairwise_decay_scores(q_ref[0, rows, dk].astype(_F32), k_ref[0, rows, dk].astype(_F32),
                                                cum_ref[rows, dk])
                sc_ref[ci * GLA_HEADS + h] = scores.astype(_BF16)

    states = [st_ref[h] for h in range(GLA_HEADS)]
    for ci in range(n_chunks):
        rows = slice(ci * c, (ci + 1) * c)
        for h in range(GLA_HEADS):
            dk, dv = head_cols(h)
            cum = cum_ref[rows, dk]
            last = cum[c - 1:c, :]
            v = v_ref[0, rows, dv]
            q_in = (q_ref[0, rows, dk].astype(_F32) * jnp.exp(cum)).astype(_BF16)
            k_out = (k_ref[0, rows, dk].astype(_F32) * jnp.exp(last - cum)).astype(_BF16)
            o = jnp.dot(sc_ref[ci * GLA_HEADS + h], v, preferred_element_type=_F32)
            o = o + lax.dot_general(q_in, states[h].astype(_BF16), (((1,), (1,)), ((), ())),
                                    preferred_element_type=_F32)
            update_t = lax.dot_general(v, k_out, (((0,), (0,)), ((), ())), preferred_element_type=_F32)
            states[h] = states[h] * jnp.exp(last) + update_t
            rb = r_ref[0, rows, dv].astype(_F32)
            o = _rms_normed(o, gn) * (rb * jax.nn.sigmoid(rb))
            o_ref[0, rows, dv] = o.astype(o_ref.dtype)
    for h in range(GLA_HEADS):
        st_ref[h] = states[h]

    @pl.when(tile == pl.num_programs(1) - 1)
    def _():
        for h in range(GLA_HEADS):
            fin_ref[0, h] = st_ref[h].T


def _gla_chunked(qb, kb, vb, gk, rb, gn):
    b, t, _ = qb.shape
    tm = GLA_ROW_TILE
    est = 2 * tm * (2 * GLA_DK * 2 + 2 * GLA_DV * 2 + GLA_DK * 4 + GLA_DV * 2) + 3 * GLA_HEADS * GLA_HEAD_DK * GLA_HEAD_DV * 4
    return pl.pallas_call(
        _gla_chunked_kernel,
        grid=(b, t // tm),
        in_specs=[
            pl.BlockSpec((1, tm, GLA_DK), lambda bi, i: (bi, i, 0)),
            pl.BlockSpec((1, tm, GLA_DK), lambda bi, i: (bi, i, 0)),
            pl.BlockSpec((1, tm, GLA_DV), lambda bi, i: (bi, i, 0)),
            pl.BlockSpec((1, tm, GLA_DK), lambda bi, i: (bi, i, 0)),
            pl.BlockSpec((1, tm, GLA_DV), lambda bi, i: (bi, i, 0)),
            pl.BlockSpec((1, GLA_HEAD_DV), lambda bi, i: (0, 0)),
        ],
        out_specs=[
            pl.BlockSpec((1, tm, GLA_DV), lambda bi, i: (bi, i, 0)),
            pl.BlockSpec((1, GLA_HEADS, GLA_HEAD_DK, GLA_HEAD_DV), lambda bi, i: (bi, 0, 0, 0)),
        ],
        out_shape=[
            jax.ShapeDtypeStruct((b, t, GLA_DV), _BF16),
            jax.ShapeDtypeStruct((b, GLA_HEADS, GLA_HEAD_DK, GLA_HEAD_DV), _F32),
        ],
        scratch_shapes=[
            pltpu.VMEM((GLA_HEADS, GLA_HEAD_DV, GLA_HEAD_DK), _F32),
            pltpu.VMEM((tm, GLA_DK), _F32),
            pltpu.VMEM((tm // GLA_CHUNK * GLA_HEADS, GLA_CHUNK, GLA_CHUNK), _BF16),
        ],
        compiler_params=pltpu.CompilerParams(
            dimension_semantics=("arbitrary", "arbitrary"), vmem_limit_bytes=_vmem_limit(est)),
        name="gla_chunked",
    )(qb, kb, vb, gk, rb, gn)


def _sample_attn_kernel(qkv_ref, c1k_ref, c1v_ref, c2k_ref, c2v_ref, c3k_ref, c3v_ref,
                        o_ref, n1k_ref, n1v_ref, n2k_ref, n2v_ref, n3k_ref, n3v_ref):
    caches = ((c1k_ref, c1v_ref, n1k_ref, n1v_ref), (c2k_ref, c2v_ref, n2k_ref, n2v_ref),
              (c3k_ref, c3v_ref, n3k_ref, n3v_ref))
    head_rows = 8
    hrow = lax.broadcasted_iota(jnp.int32, (head_rows, A_GROUP_WIDTH), 0)
    hlane = lax.broadcasted_iota(jnp.int32, (head_rows, A_GROUP_WIDTH), 1)
    own = (hlane // A_HEAD_DIM) == hrow
    stats = []
    for group, (kc_ref, vc_ref, kn_ref, vn_ref) in enumerate(caches):
        window, dil = A_GROUPS[group]
        base = group * A_GROUP_WIDTH
        q = qkv_ref[0, :, base:base + A_GROUP_WIDTH]
        k_new = qkv_ref[0, :, A_WIDTH + base:A_WIDTH + base + A_GROUP_WIDTH]
        v_new = qkv_ref[0, :, 2 * A_WIDTH + base:2 * A_WIDTH + base + A_GROUP_WIDTH]
        kc = kc_ref[0]
        vc = vc_ref[0]
        is_last = lax.broadcasted_iota(jnp.int32, (A_GROUP_WIDTH, window), 1) == window - 1
        kn_ref[0] = jnp.where(is_last, _as_column(k_new), pltpu.roll(kc, window - 1, 1))
        vn_ref[0] = jnp.where(is_last, _as_column(v_new), pltpu.roll(vc, window - 1, 1))

        q_heads = jnp.where(own, q, 0.0)
        s_old = jnp.dot(q_heads.astype(_BF16), kc.astype(_BF16), preferred_element_type=_F32)
        pos = lax.broadcasted_iota(jnp.int32, (head_rows, window), 1)
        s_old = jnp.where(pos % dil == 0, s_old, _NEG_INF)
        s_new = jnp.sum(q_heads * k_new, axis=1, keepdims=True)
        m = jnp.maximum(jnp.max(s_old, axis=1, keepdims=True), s_new)
        p_old = jnp.exp(s_old - m)
        p_new = jnp.exp(s_new - m)
        l = jnp.sum(p_old, axis=1, keepdims=True) + p_new
        num = lax.dot_general(p_old.astype(_BF16), vc.astype(_BF16), (((1,), (1,)), ((), ())),
                              preferred_element_type=_F32) + p_new * v_new
        stats.append((m, l, num))
    m_all = jnp.maximum(jnp.maximum(stats[0][0], stats[1][0]), stats[2][0])
    num = sum(jnp.exp(m - m_all) * n for m, _, n in stats)
    den = sum(jnp.exp(m - m_all) * l for m, l, _ in stats)
    o = jnp.where(own, num / den, 0.0)
    o_ref[0] = jnp.sum(o, axis=0, keepdims=True)


def _sample_attn(qkv, caches):
    nb = qkv.shape[0]
    in_specs = [pl.BlockSpec((1, 1, 3 * A_WIDTH), lambda i: (i, 0, 0))]
    out_specs = [pl.BlockSpec((1, 1, A_GROUP_WIDTH), lambda i: (i, 0, 0))]
    out_shapes = [jax.ShapeDtypeStruct((nb, 1, A_GROUP_WIDTH), _F32)]
    est = 0
    for c in caches:
        spec = pl.BlockSpec((1,) + c.shape[1:], lambda i: (i, 0, 0))
        in_specs.append(spec)
        out_specs.append(spec)
        out_shapes.append(jax.ShapeDtypeStruct(c.shape, c.dtype))
        est += 6 * c.shape[1] * c.shape[2] * 4
    return pl.pallas_call(
        _sample_attn_kernel,
        grid=(nb,),
        in_specs=in_specs,
        out_specs=out_specs,
        out_shape=out_shapes,
        compiler_params=pltpu.CompilerParams(
            dimension_semantics=("arbitrary",), vmem_limit_bytes=_vmem_limit(est)),
        name="sample_attn",
    )(qkv, *caches)


def _as_column(row_vec):
    n = row_vec.shape[1]
    eye = lax.broadcasted_iota(jnp.int32, (n, n), 0) == lax.broadcasted_iota(jnp.int32, (n, n), 1)
    return jnp.sum(jnp.where(eye, row_vec, 0.0), axis=1, keepdims=True)


def _sample_gla_kernel(q_ref, k_ref, v_ref, gk_ref, r_ref, gn_ref, st_ref, o_ref, new_ref):
    gn = gn_ref[...]
    for h in range(GLA_HEADS):
        dk = slice(h * GLA_HEAD_DK, (h + 1) * GLA_HEAD_DK)
        dv = slice(h * GLA_HEAD_DV, (h + 1) * GLA_HEAD_DV)
        decay = _as_column(jnp.exp(gk_ref[0, :, dk]))
        k_col = _as_column(k_ref[0, :, dk])
        q_col = _as_column(q_ref[0, :, dk])
        state = decay * st_ref[0, h] + k_col * v_ref[0, :, dv]
        new_ref[0, h] = state
        o = jnp.sum(q_col * state, axis=0, keepdims=True)
        rb = r_ref[0, :, dv]
        o_ref[0, :, dv] = _rms_normed(o, gn) * (rb * jax.nn.sigmoid(rb))


def _sample_gla(qb, kb, vb, gk, rb, gn, state):
    nb = qb.shape[0]

    def row_spec(width):
        return pl.BlockSpec((1, 1, width), lambda i: (i, 0, 0))

    state_spec = pl.BlockSpec((1,) + state.shape[1:], lambda i: (i, 0, 0, 0))
    return pl.pallas_call(
        _sample_gla_kernel,
        grid=(nb,),
        in_specs=[row_spec(GLA_DK), row_spec(GLA_DK), row_spec(GLA_DV), row_spec(GLA_DK), row_spec(GLA_DV),
                  pl.BlockSpec((1, GLA_HEAD_DV), lambda i: (0, 0)), state_spec],
        out_specs=[row_spec(GLA_DV), state_spec],
        out_shape=[jax.ShapeDtypeStruct((nb, 1, GLA_DV), _F32), jax.ShapeDtypeStruct(state.shape, state.dtype)],
        compiler_params=pltpu.CompilerParams(dimension_semantics=("arbitrary",)),
        name="sample_gla",
    )(qb, kb, vb, gk, rb, gn, state)


def _merge_ffn_kernel(x_ref, oa_ref, ob_ref, ga_ref, gb_ref, wpa_ref, wpb_ref, wo_ref, g2_ref, wup_ref, wdn_ref,
                      gf_ref, y_ref):
    tm = x_ref.shape[0]
    sub = min(tm, ROW_SUBTILE)
    for r0 in range(0, tm, sub):
        rows = slice(r0, r0 + sub)
        ya = jnp.dot(oa_ref[rows, :].astype(_BF16), wpa_ref[...], preferred_element_type=_F32)
        yb = jnp.dot(ob_ref[rows, :].astype(_BF16), wpb_ref[...], preferred_element_type=_F32)
        mix = (jax.nn.sigmoid(ga_ref[rows, :].astype(_F32)) * ya
               + jax.nn.sigmoid(gb_ref[rows, :].astype(_F32)) * yb)
        h = x_ref[rows, :] + jnp.dot(mix.astype(_BF16), wo_ref[...], preferred_element_type=_F32)
        hn = _rms_normed(h, g2_ref[...]).astype(_BF16)
        acc = h
        for c0 in range(0, D_FF, FF_CHUNK):
            f = jnp.maximum(jnp.dot(hn, wup_ref[:, c0:c0 + FF_CHUNK], preferred_element_type=_F32), 0.0)
            acc = acc + jnp.dot((f * f).astype(_BF16), wdn_ref[c0:c0 + FF_CHUNK, :], preferred_element_type=_F32)
        y_ref[rows, :] = _rms_normed(acc, gf_ref[...])


def _merge_ffn(x, oa, ob, ga, gb, wpa, wpb, wo, g2, wup, wdn, gf, tm):
    m, d = x.shape

    def rows(width):
        return pl.BlockSpec((tm, width), lambda i: (i, 0))

    weights = (wpa, wpb, wo, wup, wdn)
    est = (sum(w.size for w in weights) * 2 + 2 * tm * (2 * d * 4 + A_GROUP_WIDTH * 4 + 3 * d * 4)
           + ROW_SUBTILE * (6 * d + 2 * FF_CHUNK) * 4)
    return pl.pallas_call(
        _merge_ffn_kernel,
        grid=(m // tm,),
        in_specs=[rows(d), rows(A_GROUP_WIDTH), rows(GLA_DV), rows(d), rows(d),
                  _resident(wpa.shape), _resident(wpb.shape), _resident(wo.shape), _resident((1, d)),
                  _resident(wup.shape), _resident(wdn.shape), _resident((1, d))],
        out_specs=rows(d),
        out_shape=jax.ShapeDtypeStruct((m, d), _F32),
        compiler_params=pltpu.CompilerParams(
            dimension_semantics=("arbitrary",), vmem_limit_bytes=_vmem_limit(est)),
        name="merge_ffn",
    )(x, oa, ob, ga, gb, wpa, wpb, wo, g2, wup, wdn, gf)


def _split_w_in(w_in):
    sizes = (A_WIDTH, A_WIDTH, A_WIDTH, GLA_DK, GLA_DK, GLA_DV, GLA_DV, GLA_GATE_RANK, D_MODEL, D_MODEL)
    offs = np.concatenate([[0], np.cumsum(sizes)])
    qa, ka, va, qb, kb, vb, rb, glr, ga, gb = (w_in[:, offs[j]:offs[j + 1]] for j in range(len(sizes)))
    glr = jnp.pad(glr, ((0, 0), (0, LANES - GLA_GATE_RANK)))
    w_attn = jnp.concatenate([qa, ka, va], axis=1).astype(_BF16)
    w_gla = jnp.concatenate([qb, kb, vb, rb, ga, gb, glr], axis=1).astype(_BF16)
    return w_attn, w_gla


def kernel(x_prompt, x_sample, cache_a1_k, cache_a1_v, cache_a2_k, cache_a2_v, cache_a3_k, cache_a3_v,
           state_gla, g_norm1, w_in, w_gk2, b_gk, g_gla, w_pa, w_pb, w_o, g_norm2, w_up, w_down, g_final):
    assert g_norm1.shape[0] == 1, "one layer"
    b, t, d = x_prompt.shape
    nb = x_sample.shape[0]
    assert x_sample.shape[1] == 1, "one new token per sample row"

    w_attn, w_gla = _split_w_in(w_in[0])
    w_gk = jnp.pad(w_gk2[0], ((0, LANES - GLA_GATE_RANK), (0, 0))).astype(_BF16)
    b_gk_row = b_gk[0][None, :]
    g1 = g_norm1[0][None, :]
    g2 = g_norm2[0][None, :]
    gf = g_final[None, :]
    gn = g_gla[0][None, :]
    wpa = w_pa[0].astype(_BF16)
    wpb = w_pb[0].astype(_BF16)
    wo = w_o[0].astype(_BF16)
    wup = w_up[0].astype(_BF16)
    wdn = w_down[0].astype(_BF16)

    cos_p, sin_p = _rope_tables(t, 0, 1)
    outs = _attn_proj_prompt(x_prompt, g1, w_attn, cos_p, sin_p)
    packs, p_caches = outs[:3], outs[3:]
    xp = x_prompt.reshape(b * t, d)
    qb, kb, vb, rb, ga, gb, gk = _gla_proj(xp, g1, w_gla, w_gk, b_gk_row, ROW_TILE, _BF16)
    oa = _dilated_attn(packs, t)

    def seq(z):
        return z.reshape(b, t, z.shape[-1])

    ob, p_gla = _gla_chunked(seq(qb), seq(kb), seq(vb), seq(gk), seq(rb), gn)
    y_prompt = _merge_ffn(xp, oa.reshape(b * t, A_GROUP_WIDTH), ob.reshape(b * t, GLA_DV), ga, gb,
                          wpa, wpb, wo, g2, wup, wdn, gf, ROW_TILE).reshape(b, t, d)

    cos_s, sin_s = _rope_tables(nb, PAST_LEN, 0)
    qkv_s = _attn_proj_sample(x_sample.reshape(1, nb, d), g1, w_attn, cos_s, sin_s).reshape(nb, 1, 3 * A_WIDTH)
    xs = x_sample.reshape(nb, d)
    qb_s, kb_s, vb_s, rb_s, ga_s, gb_s, gk_s = _gla_proj(xs, g1, w_gla, w_gk, b_gk_row, nb, _F32)
    caches = [jnp.transpose(c[0], (0, 2, 3, 1)).reshape(nb, A_GROUP_WIDTH, c.shape[2])
              for c in (cache_a1_k, cache_a1_v, cache_a2_k, cache_a2_v, cache_a3_k, cache_a3_v)]
    s_outs = _sample_attn(qkv_s, caches)
    oa_s, s_caches = s_outs[0], s_outs[1:]

    def tok(z):
        return z.reshape(nb, 1, z.shape[-1])

    ob_s, s_gla = _sample_gla(tok(qb_s), tok(kb_s), tok(vb_s), tok(gk_s), tok(rb_s), gn, state_gla[0])
    y_sample = _merge_ffn(xs, oa_s.reshape(nb, A_GROUP_WIDTH), ob_s.reshape(nb, GLA_DV), ga_s, gb_s,
                          wpa, wpb, wo, g2, wup, wdn, gf, nb).reshape(nb, 1, d)

    def cache_out(z):
        z = z.reshape(z.shape[0], A_HEADS_PER_GROUP, A_HEAD_DIM, z.shape[2])
        return jnp.transpose(z, (0, 3, 1, 2))[None]

    return (y_prompt, y_sample, *[cache_out(z) for z in p_caches], p_gla[None],
            *[cache_out(z) for z in s_caches], s_gla[None])
```

```python
import functools

import jax
import jax.numpy as jnp
import numpy as np
from jax import lax
from jax.experimental import pallas as pl
from jax.experimental.pallas import tpu as pltpu

D_MODEL = 1024
PAST_LEN = 16384
A_GROUPS = ((128, 1), (512, 4), (2048, 16))
A_N_GROUPS = 3
A_HEADS_PER_GROUP = 4
A_HEAD_DIM = 64
A_GROUP_WIDTH = A_HEADS_PER_GROUP * A_HEAD_DIM
A_WIDTH = A_N_GROUPS * A_GROUP_WIDTH
A_WINDOW_KEYS = 128
ROPE_THETA = 10000.0
GLA_HEADS = 4
GLA_HEAD_DK = 128
GLA_HEAD_DV = 256
GLA_DK = GLA_HEADS * GLA_HEAD_DK
GLA_DV = GLA_HEADS * GLA_HEAD_DV
GLA_GATE_RANK = 16
GLA_GATE_NORM = 16.0
D_FF = 4 * D_MODEL
EPS = 1e-6
LOG2_E = 1.4426950408889634

LANES = 128
V7X_SCOPED_VMEM_CAP = 56 * 1024 * 1024

ROW_TILE = 512
ROW_SUBTILE = 256
Q_BLOCK = 128
ATTN_BLOCKS_PER_STEP = 16
GLA_CHUNK = 128
GLA_ROW_TILE = 512
GLA_FACTOR_RANGE = 40.0
FF_CHUNK = 1024

_F32 = jnp.float32
_BF16 = jnp.bfloat16
_NEG_INF = float("-inf")


def _vmem_limit(nbytes):
    return int(min(V7X_SCOPED_VMEM_CAP, max(16 * 1024 * 1024, nbytes * 3 // 2)))


def _resident(shape):
    nd = len(shape)
    return pl.BlockSpec(shape, lambda *_: (0,) * nd, pipeline_mode=pl.Buffered(1))


def _rms_normed(x, g):
    return x * lax.rsqrt(jnp.mean(x * x, axis=-1, keepdims=True) + EPS) * g


def _rope_table_kernel(inv_ref, sign_ref, cos_ref, sin_ref, *, pos0, pos_step):
    rows = cos_ref.shape[0]
    row = lax.broadcasted_iota(jnp.int32, (rows, LANES), 0) + pl.program_id(0) * rows
    pos = (pos0 + row * pos_step).astype(_F32)
    ang = pos * inv_ref[...]
    cos_ref[...] = jnp.cos(ang)
    sin_ref[...] = jnp.sin(ang) * sign_ref[...]


def _rope_tables(n_rows, pos0, pos_step):
    half = A_HEAD_DIM // 2
    lane = jnp.arange(LANES)
    inv = ROPE_THETA ** (-((lane % half).astype(_F32)) / half)
    sign = jnp.where((lane % A_HEAD_DIM) < half, -1.0, 1.0).astype(_F32)
    tile = min(n_rows, ROW_TILE)
    return pl.pallas_call(
        functools.partial(_rope_table_kernel, pos0=pos0, pos_step=pos_step),
        grid=(n_rows // tile,),
        in_specs=[pl.BlockSpec((1, LANES), lambda i: (0, 0))] * 2,
        out_specs=[pl.BlockSpec((tile, LANES), lambda i: (i, 0))] * 2,
        out_shape=[jax.ShapeDtypeStruct((n_rows, LANES), _F32)] * 2,
        name="rope_table",
    )(inv[None, :], sign[None, :])


def _rope_slab(z, cos, sin_signed, first_half):
    half = A_HEAD_DIM // 2
    partner = jnp.where(first_half, pltpu.roll(z, LANES - half, 1), pltpu.roll(z, half, 1))
    return z * cos + partner * sin_signed


_N_SLABS = 3 * A_WIDTH // LANES
_SLABS_PER_SECTION = A_WIDTH // LANES
_SLABS_PER_GROUP = A_GROUP_WIDTH // LANES


def _attn_proj_slabs(u, w_ref, cos, sin, q_scale):
    y = jnp.dot(u, w_ref[...], preferred_element_type=_F32)
    lane = lax.broadcasted_iota(jnp.int32, cos.shape, 1)
    first_half = (lane % A_HEAD_DIM) < (A_HEAD_DIM // 2)
    for s in range(_N_SLABS):
        z = y[:, s * LANES:(s + 1) * LANES]
        if s < _SLABS_PER_SECTION:
            z = _rope_slab(z, cos, sin, first_half) * q_scale
        elif s < 2 * _SLABS_PER_SECTION:
            z = _rope_slab(z, cos, sin, first_half)
        yield s, z


def _attn_proj_prompt_kernel(x_ref, g_ref, w_ref, cos_ref, sin_ref,
                             u_ref, p1_ref, p2_ref, p3_ref, c1k_ref, c1v_ref, c2k_ref, c2v_ref, c3k_ref, c3v_ref,
                             slab_ref, *, tm, n_tiles):
    i = pl.program_id(1)
    packs = (p1_ref, p2_ref, p3_ref)
    caches = ((c1k_ref, c1v_ref), (c2k_ref, c2v_ref), (c3k_ref, c3v_ref))
    sub = min(tm, ROW_SUBTILE)
    for r0 in range(0, tm, sub):
        rows = slice(r0, r0 + sub)
        u = _rms_normed(x_ref[0, rows, :], g_ref[...]).astype(_BF16)
        u_ref[0, rows, :] = u
        for s, z in _attn_proj_slabs(u, w_ref, cos_ref[rows, :], sin_ref[rows, :], A_HEAD_DIM ** -0.5 * LOG2_E):
            slab_ref[s, rows, :] = z
            section, within = divmod(s, _SLABS_PER_SECTION)
            group, pair = divmod(within, _SLABS_PER_GROUP)
            dil = A_GROUPS[group][1]
            col = (section * _SLABS_PER_GROUP + pair) * LANES
            for r in range(dil):
                picked = slab_ref[s, pl.ds(r0 + r, sub // dil, stride=dil), :] if dil > 1 else z
                packs[group][0, r, r0 // dil:(r0 + sub) // dil, col:col + LANES] = picked.astype(_BF16)

    for group, (window, _) in enumerate(A_GROUPS):
        keep = min(window, tm)
        first_kept_tile = n_tiles - max(window // tm, 1)

        @pl.when(i >= first_kept_tile)
        def _(group=group, keep=keep):
            for section in (1, 2):
                for pair in range(_SLABS_PER_GROUP):
                    s = section * _SLABS_PER_SECTION + group * _SLABS_PER_GROUP + pair
                    caches[group][section - 1][0, :, pair * LANES:(pair + 1) * LANES] = slab_ref[s, tm - keep:tm, :]


def _attn_proj_sample_kernel(x_ref, g_ref, w_ref, cos_ref, sin_ref, o_ref):
    u = _rms_normed(x_ref[0], g_ref[...]).astype(_BF16)
    for s, z in _attn_proj_slabs(u, w_ref, cos_ref[...], sin_ref[...], A_HEAD_DIM ** -0.5):
        o_ref[0, :, s * LANES:(s + 1) * LANES] = z


def _attn_proj_prompt(x, g, w, cos, sin):
    b, t, d = x.shape
    tm = ROW_TILE
    n_tiles = t // tm
    pack_shapes, pack_specs = [], []
    for _, dil in A_GROUPS:
        pack_shapes.append(jax.ShapeDtypeStruct((b, dil, t // dil, 3 * A_GROUP_WIDTH), _BF16))
        pack_specs.append(pl.BlockSpec((1, dil, tm // dil, 3 * A_GROUP_WIDTH), lambda bi, i: (bi, 0, i, 0)))
    cache_shapes, cache_specs = [], []
    for window, _ in A_GROUPS:
        keep = min(window, t)
        blk = min(keep, tm)
        first = n_tiles - max(keep // tm, 1)
        for _ in range(2):
            cache_shapes.append(jax.ShapeDtypeStruct((b, keep, A_GROUP_WIDTH), _F32))
            cache_specs.append(pl.BlockSpec(
                (1, blk, A_GROUP_WIDTH), lambda bi, i, first=first: (bi, jnp.maximum(i - first, 0), 0)))
    est = (w.size * 2 + 2 * tm * d * 4 + 2 * 3 * tm * 3 * A_GROUP_WIDTH * 2 + 2 * 6 * tm * A_GROUP_WIDTH * 4
           + 2 * tm * 3 * A_WIDTH * 4)
    return pl.pallas_call(
        functools.partial(_attn_proj_prompt_kernel, tm=tm, n_tiles=n_tiles),
        grid=(b, n_tiles),
        in_specs=[
            pl.BlockSpec((1, tm, d), lambda bi, i: (bi, i, 0)),
            _resident((1, d)),
            _resident(w.shape),
            pl.BlockSpec((tm, LANES), lambda bi, i: (i, 0)),
            pl.BlockSpec((tm, LANES), lambda bi, i: (i, 0)),
        ],
        out_specs=[pl.BlockSpec((1, tm, d), lambda bi, i: (bi, i, 0))] + pack_specs + cache_specs,
        out_shape=[jax.ShapeDtypeStruct((b, t, d), _BF16)] + pack_shapes + cache_shapes,
        scratch_shapes=[pltpu.VMEM((_N_SLABS, tm, LANES), _F32)],
        compiler_params=pltpu.CompilerParams(
            dimension_semantics=("arbitrary", "arbitrary"), vmem_limit_bytes=_vmem_limit(est)),
        name="attn_proj_prompt",
    )(x, g, w, cos, sin)


def _attn_proj_sample(x, g, w, cos, sin):
    _, m, d = x.shape
    est = w.size * 2 * 2 + 4 * m * 3 * A_WIDTH * 4
    return pl.pallas_call(
        _attn_proj_sample_kernel,
        grid=(1,),
        in_specs=[
            pl.BlockSpec((1, m, d), lambda i: (0, 0, 0)),
            pl.BlockSpec((1, d), lambda i: (0, 0)),
            pl.BlockSpec(w.shape, lambda i: (0, 0)),
            pl.BlockSpec((m, LANES), lambda i: (0, 0)),
            pl.BlockSpec((m, LANES), lambda i: (0, 0)),
        ],
        out_specs=pl.BlockSpec((1, m, 3 * A_WIDTH), lambda i: (0, 0, 0)),
        out_shape=jax.ShapeDtypeStruct((1, m, 3 * A_WIDTH), _F32),
        compiler_params=pltpu.CompilerParams(vmem_limit_bytes=_vmem_limit(est)),
        name="attn_proj_sample",
    )(x, g, w, cos, sin)


_GP_SECTIONS = {}
_off = 0
for _name, _width in (("qb", GLA_DK), ("kb", GLA_DK), ("vb", GLA_DV), ("rb", GLA_DV),
                      ("ga", D_MODEL), ("gb", D_MODEL), ("glr", LANES)):
    _GP_SECTIONS[_name] = (_off, _width)
    _off += _width
_GP_WIDTH = _off


def _gla_proj_kernel(x_ref, g_ref, w_ref, wgk_ref, bgk_ref,
                     qb_ref, kb_ref, vb_ref, rb_ref, ga_ref, gb_ref, gk_ref, *, input_is_normed):
    tm = x_ref.shape[0]
    sub = min(tm, ROW_SUBTILE)
    for r0 in range(0, tm, sub):
        rows = slice(r0, r0 + sub)
        if input_is_normed:
            u = x_ref[rows, :]
        else:
            u = _rms_normed(x_ref[rows, :], g_ref[...]).astype(_BF16)

        def section(name, u=u):
            off, width = _GP_SECTIONS[name]
            return jnp.dot(u, w_ref[:, off:off + width], preferred_element_type=_F32)

        glr = section("glr").astype(_BF16)
        z = jnp.dot(glr, wgk_ref[...], preferred_element_type=_F32) + bgk_ref[...]
        log_sigmoid = jnp.minimum(z, 0.0) - jnp.log(1.0 + jnp.exp(-jnp.abs(z)))
        gk_ref[rows, :] = log_sigmoid / GLA_GATE_NORM
        qb_ref[rows, :] = (section("qb") * (GLA_HEAD_DK ** -0.5)).astype(qb_ref.dtype)
        kb_ref[rows, :] = section("kb").astype(kb_ref.dtype)
        vb_ref[rows, :] = section("vb").astype(vb_ref.dtype)
        rb_ref[rows, :] = section("rb").astype(rb_ref.dtype)
        ga_ref[rows, :] = section("ga").astype(ga_ref.dtype)
        gb_ref[rows, :] = section("gb").astype(gb_ref.dtype)


def _gla_proj(x, g, w, wgk, bgk, tm, act_dtype):
    m, d = x.shape
    input_is_normed = x.dtype == _BF16
    widths = (GLA_DK, GLA_DK, GLA_DV, GLA_DV, D_MODEL, D_MODEL)
    out_shapes = [jax.ShapeDtypeStruct((m, wd), act_dtype) for wd in widths]
    out_shapes.append(jax.ShapeDtypeStruct((m, GLA_DK), _F32))
    out_specs = [pl.BlockSpec((tm, wd), lambda i: (i, 0)) for wd in widths + (GLA_DK,)]
    est = w.size * 2 + 2 * tm * d * 4 + 2 * tm * (sum(widths) + GLA_DK) * 4 + tm * GLA_DV * 4 * 2
    return pl.pallas_call(
        functools.partial(_gla_proj_kernel, input_is_normed=input_is_normed),
        grid=(m // tm,),
        in_specs=[
            pl.BlockSpec((tm, d), lambda i: (i, 0)),
            _resident((1, d)),
            _resident(w.shape),
            _resident(wgk.shape),
            _resident(bgk.shape),
        ],
        out_specs=out_specs,
        out_shape=out_shapes,
        compiler_params=pltpu.CompilerParams(
            dimension_semantics=("arbitrary",), vmem_limit_bytes=_vmem_limit(est)),
        name="gla_proj",
    )(x, g, w, wgk, bgk)


def _window_attention(q, k, v, bias, head0):
    qb = Q_BLOCK
    zero = jnp.zeros_like(q)
    q_heads = jnp.concatenate([jnp.where(head0, q, zero), jnp.where(head0, zero, q)], axis=0)
    s = lax.dot_general(q_heads, k, (((1,), (1,)), ((), ())), preferred_element_type=_F32)
    s = s + jnp.concatenate([bias, bias], axis=0)
    m = jnp.max(s, axis=1, keepdims=True)
    p = jnp.exp2(s - m)
    l = jnp.sum(p, axis=1, keepdims=True)
    num = jnp.dot(p.astype(_BF16), v, preferred_element_type=_F32)
    return (jnp.where(head0, m[:qb], m[qb:]), jnp.where(head0, l[:qb], l[qb:]),
            jnp.where(head0, num[:qb], num[qb:]))


def _dilated_attn_kernel(q1_ref, k1_ref, v1_ref, q2_ref, k2_ref, v2_ref, q3_ref, k3_ref, v3_ref,
                         o_ref, m_ref, l_ref, n_ref, bias_ref):
    qb = Q_BLOCK
    lane = lax.broadcasted_iota(jnp.int32, (qb, LANES), 1)
    head0 = lane < A_HEAD_DIM
    row = lax.broadcasted_iota(jnp.int32, (qb, 2 * qb), 0)
    col = lax.broadcasted_iota(jnp.int32, (qb, 2 * qb), 1)
    for j, offset in enumerate((0, qb)):
        diff = row - col + offset
        bias_ref[j] = jnp.where((diff >= 0) & (diff <= A_WINDOW_KEYS), 0.0, _NEG_INF)
    groups = ((q1_ref, k1_ref, v1_ref), (q2_ref, k2_ref, v2_ref), (q3_ref, k3_ref, v3_ref))
    first_group, last_group = A_N_GROUPS - 1, 0
    per_step = ATTN_BLOCKS_PER_STEP

    for group in range(A_N_GROUPS - 1, -1, -1):
        q_ref, k_ref, v_ref = groups[group]
        dil = A_GROUPS[group][1]
        n_blocks = q_ref.shape[2] // qb
        res_per_step = min(dil, per_step)
        blk_per_step = per_step // res_per_step
        blk_steps = n_blocks // blk_per_step

        def step(idx, carry, q_ref=q_ref, k_ref=k_ref, v_ref=v_ref, dil=dil, group=group,
                 res_per_step=res_per_step, blk_per_step=blk_per_step, blk_steps=blk_steps):
            stats, rows = [], []
            for u in range(per_step):
                if dil == 1:
                    r = 0
                else:
                    r = (idx // blk_steps) * res_per_step + u % res_per_step
                n = (idx % blk_steps) * blk_per_step + u // res_per_step
                q0 = pl.multiple_of(n * qb, qb)
                k0 = pl.multiple_of(jnp.maximum(n - 1, 0) * qb, qb)
                q = q_ref[0, r, pl.ds(q0, qb), :]
                k = k_ref[0, r, pl.ds(k0, 2 * qb), :]
                v = v_ref[0, r, pl.ds(k0, 2 * qb), :]
                stats.append(_window_attention(q, k, v, bias_ref[jnp.minimum(n, 1)], head0))
                rows.append(pl.ds(q0, qb) if dil == 1 else pl.ds(q0 * dil + r, qb, stride=dil))
            if group == first_group:
                for (m_g, l_g, n_g), rw in zip(stats, rows):
                    m_ref[rw, :] = m_g
                    l_ref[rw, :] = l_g
                    n_ref[rw, :] = n_g
                return carry
            old = [(m_ref[rw, :], l_ref[rw, :], n_ref[rw, :]) for rw in rows]
            for (m_g, l_g, n_g), (m_old, l_old, n_old), rw in zip(stats, old, rows):
                m_new = jnp.maximum(m_old, m_g)
                a = jnp.exp2(m_old - m_new)
                b = jnp.exp2(m_g - m_new)
                l_new = a * l_old + b * l_g
                n_new = a * n_old + b * n_g
                if group == last_group:
                    o_ref[0, rw, :] = (n_new / l_new).astype(o_ref.dtype)
                else:
                    m_ref[rw, :] = m_new
                    l_ref[rw, :] = l_new
                    n_ref[rw, :] = n_new
            return carry

        lax.fori_loop(0, dil * n_blocks // per_step, step, 0)


def _dilated_attn(packs, t):
    b = packs[0].shape[0]
    n_pairs = A_GROUP_WIDTH // LANES
    in_specs, operands = [], []
    for pack, (_, dil) in zip(packs, A_GROUPS):
        for section in range(3):
            in_specs.append(pl.BlockSpec(
                (1, dil, t // dil, LANES), lambda bi, p, section=section: (bi, 0, 0, section * n_pairs + p)))
            operands.append(pack)
    est = 2 * 9 * t * LANES * 2 + 2 * t * LANES * 2 + 3 * t * LANES * 4
    return pl.pallas_call(
        _dilated_attn_kernel,
        grid=(b, n_pairs),
        in_specs=in_specs,
        out_specs=pl.BlockSpec((1, t, LANES), lambda bi, p: (bi, 0, p)),
        out_shape=jax.ShapeDtypeStruct((b, t, A_GROUP_WIDTH), _BF16),
        scratch_shapes=[pltpu.VMEM((t, LANES), _F32)] * 3 + [pltpu.VMEM((2, Q_BLOCK, 2 * Q_BLOCK), _F32)],
        compiler_params=pltpu.CompilerParams(
            dimension_semantics=("arbitrary", "arbitrary"), vmem_limit_bytes=_vmem_limit(est)),
        name="dilated_attn",
    )(*operands)


def _pairwise_decay_scores(q, k, cum):
    c = q.shape[0]
    row = lax.broadcasted_iota(jnp.int32, (c, c), 0)
    col = lax.broadcasted_iota(jnp.int32, (c, c), 1)
    row_d = lax.broadcasted_iota(jnp.int32, q.shape, 0)

    def sub_diagonal(delta, acc):
        k_back = pltpu.roll(k, delta, 0)
        cum_back = pltpu.roll(cum, delta, 0)
        log_decay = jnp.where(row_d >= delta, cum - cum_back, 0.0)
        diag = jnp.sum(q * k_back * jnp.exp2(log_decay), axis=1, keepdims=True)
        return acc + jnp.where(row - col == delta, diag, 0.0)

    return lax.fori_loop(0, c, sub_diagonal, jnp.zeros((c, c), _F32))


def _gla_chunked_kernel(q_ref, k_ref, v_ref, gk_ref, r_ref, gn_ref, o_ref, fin_ref, st_ref, cum_ref, sc_ref):
    c = GLA_CHUNK
    n_chunks = q_ref.shape[1] // c
    tile = pl.program_id(1)

    @pl.when(tile == 0)
    def _():
        st_ref[...] = jnp.zeros_like(st_ref)

    row = lax.broadcasted_iota(jnp.int32, (c, c), 0)
    col = lax.broadcasted_iota(jnp.int32, (c, c), 1)
    causal = row >= col
    causal_bf = jnp.where(causal, 1.0, 0.0).astype(_BF16)
    gn = gn_ref[...]

    def head_cols(h):
        return (slice(h * GLA_HEAD_DK, (h + 1) * GLA_HEAD_DK), slice(h * GLA_HEAD_DV, (h + 1) * GLA_HEAD_DV))

    spread = jnp.zeros((1, GLA_DK), _F32)
    for ci in range(n_chunks):
        rows = slice(ci * c, (ci + 1) * c)
        g = gk_ref[0, rows, :]
        g_hi = g.astype(_BF16)
        g_lo = (g - g_hi.astype(_F32)).astype(_BF16)
        parts = jnp.dot(causal_bf, jnp.concatenate([g_hi, g_lo], axis=1), preferred_element_type=_F32)
        cum = (parts[:, :GLA_DK] + parts[:, GLA_DK:]) * LOG2_E
        cum_ref[rows, :] = cum
        mid = cum[c // 2 - 1:c // 2, :]
        spread = jnp.maximum(spread, jnp.maximum(cum[0:1, :] - mid, mid - cum[c - 1:c, :]))
    factorable = jnp.max(spread) <= GLA_FACTOR_RANGE * LOG2_E

    @pl.when(factorable)
    def _():
        for ci in range(n_chunks):
            rows = slice(ci * c, (ci + 1) * c)
            for h in range(GLA_HEADS):
                dk, _ = head_cols(h)
                cum = cum_ref[rows, dk]
                mid = cum[c // 2 - 1:c // 2, :]
                q_mid = (q_ref[0, rows, dk].astype(_F32) * jnp.exp2(cum - mid)).astype(_BF16)
                k_mid = (k_ref[0, rows, dk].astype(_F32) * jnp.exp2(mid - cum)).astype(_BF16)
                scores = lax.dot_general(q_mid, k_mid, (((1,), (1,)), ((), ())), preferred_element_type=_F32)
                sc_ref[ci * GLA_HEADS + h] = jnp.where(causal, scores, 0.0).astype(_BF16)

    @pl.when(jnp.logical_not(factorable))
    def _():
        for ci in range(n_chunks):
            rows = slice(ci * c, (ci + 1) * c)
            for h in range(GLA_HEADS):
                dk, _ = head_cols(h)
                scores = _pairwise_decay_scores(q_ref[0, rows, dk].astype(_F32), k_ref[0, rows, dk].astype(_F32),
                                                cum_ref[rows, dk])
                sc_ref[ci * GLA_HEADS + h] = scores.astype(_BF16)

    states = [st_ref[h] for h in range(GLA_HEADS)]
    for ci in range(n_chunks):
        rows = slice(ci * c, (ci + 1) * c)
        for h in range(GLA_HEADS):
            dk, dv = head_cols(h)
            cum = cum_ref[rows, dk]
            last = cum[c - 1:c, :]
            v = v_ref[0, rows, dv]
            q_in = (q_ref[0, rows, dk].astype(_F32) * jnp.exp2(cum)).astype(_BF16)
            k_out = (k_ref[0, rows, dk].astype(_F32) * jnp.exp2(last - cum)).astype(_BF16)
            o = jnp.dot(sc_ref[ci * GLA_HEADS + h], v, preferred_element_type=_F32)
            o = o + lax.dot_general(q_in, states[h].astype(_BF16), (((1,), (1,)), ((), ())),
                                    preferred_element_type=_F32)
            update_t = lax.dot_general(v, k_out, (((0,), (0,)), ((), ())), preferred_element_type=_F32)
            states[h] = states[h] * jnp.exp2(last) + update_t
            rb = r_ref[0, rows, dv].astype(_F32)
            o = _rms_normed(o, gn) * (rb * jax.nn.sigmoid(rb))
            o_ref[0, rows, dv] = o.astype(o_ref.dtype)
    for h in range(GLA_HEADS):
        st_ref[h] = states[h]

    @pl.when(tile == pl.num_programs(1) - 1)
    def _():
        for h in range(GLA_HEADS):
            fin_ref[0, h] = st_ref[h].T


def _gla_chunked(qb, kb, vb, gk, rb, gn):
    b, t, _ = qb.shape
    tm = GLA_ROW_TILE
    est = 2 * tm * (2 * GLA_DK * 2 + 2 * GLA_DV * 2 + GLA_DK * 4 + GLA_DV * 2) + 3 * GLA_HEADS * GLA_HEAD_DK * GLA_HEAD_DV * 4
    return pl.pallas_call(
        _gla_chunked_kernel,
        grid=(b, t // tm),
        in_specs=[
            pl.BlockSpec((1, tm, GLA_DK), lambda bi, i: (bi, i, 0)),
            pl.BlockSpec((1, tm, GLA_DK), lambda bi, i: (bi, i, 0)),
            pl.BlockSpec((1, tm, GLA_DV), lambda bi, i: (bi, i, 0)),
            pl.BlockSpec((1, tm, GLA_DK), lambda bi, i: (bi, i, 0)),
            pl.BlockSpec((1, tm, GLA_DV), lambda bi, i: (bi, i, 0)),
            pl.BlockSpec((1, GLA_HEAD_DV), lambda bi, i: (0, 0)),
        ],
        out_specs=[
            pl.BlockSpec((1, tm, GLA_DV), lambda bi, i: (bi, i, 0)),
            pl.BlockSpec((1, GLA_HEADS, GLA_HEAD_DK, GLA_HEAD_DV), lambda bi, i: (bi, 0, 0, 0)),
        ],
        out_shape=[
            jax.ShapeDtypeStruct((b, t, GLA_DV), _BF16),
            jax.ShapeDtypeStruct((b, GLA_HEADS, GLA_HEAD_DK, GLA_HEAD_DV), _F32),
        ],
        scratch_shapes=[
            pltpu.VMEM((GLA_HEADS, GLA_HEAD_DV, GLA_HEAD_DK), _F32),
            pltpu.VMEM((tm, GLA_DK), _F32),
            pltpu.VMEM((tm // GLA_CHUNK * GLA_HEADS, GLA_CHUNK, GLA_CHUNK), _BF16),
        ],
        compiler_params=pltpu.CompilerParams(
            dimension_semantics=("arbitrary", "arbitrary"), vmem_limit_bytes=_vmem_limit(est)),
        name="gla_chunked",
    )(qb, kb, vb, gk, rb, gn)


def _sample_attn_kernel(qkv_ref, c1k_ref, c1v_ref, c2k_ref, c2v_ref, c3k_ref, c3v_ref,
                        o_ref, n1k_ref, n1v_ref, n2k_ref, n2v_ref, n3k_ref, n3v_ref):
    caches = ((c1k_ref, c1v_ref, n1k_ref, n1v_ref), (c2k_ref, c2v_ref, n2k_ref, n2v_ref),
              (c3k_ref, c3v_ref, n3k_ref, n3v_ref))
    head_rows = 8
    hrow = lax.broadcasted_iota(jnp.int32, (head_rows, A_GROUP_WIDTH), 0)
    hlane = lax.broadcasted_iota(jnp.int32, (head_rows, A_GROUP_WIDTH), 1)
    own = (hlane // A_HEAD_DIM) == hrow
    stats = []
    for group, (kc_ref, vc_ref, kn_ref, vn_ref) in enumerate(caches):
        window, dil = A_GROUPS[group]
        base = group * A_GROUP_WIDTH
        q = qkv_ref[0, :, base:base + A_GROUP_WIDTH]
        k_new = qkv_ref[0, :, A_WIDTH + base:A_WIDTH + base + A_GROUP_WIDTH]
        v_new = qkv_ref[0, :, 2 * A_WIDTH + base:2 * A_WIDTH + base + A_GROUP_WIDTH]
        kc = kc_ref[0]
        vc = vc_ref[0]
        is_last = lax.broadcasted_iota(jnp.int32, (A_GROUP_WIDTH, window), 1) == window - 1
        kn_ref[0] = jnp.where(is_last, _as_column(k_new), pltpu.roll(kc, window - 1, 1))
        vn_ref[0] = jnp.where(is_last, _as_column(v_new), pltpu.roll(vc, window - 1, 1))

        q_heads = jnp.where(own, q, 0.0)
        s_old = jnp.dot(q_heads.astype(_BF16), kc.astype(_BF16), preferred_element_type=_F32)
        pos = lax.broadcasted_iota(jnp.int32, (head_rows, window), 1)
        s_old = jnp.where(pos % dil == 0, s_old, _NEG_INF)
        s_new = jnp.sum(q_heads * k_new, axis=1, keepdims=True)
        m = jnp.maximum(jnp.max(s_old, axis=1, keepdims=True), s_new)
        p_old = jnp.exp(s_old - m)
        p_new = jnp.exp(s_new - m)
        l = jnp.sum(p_old, axis=1, keepdims=True) + p_new
        num = lax.dot_general(p_old.astype(_BF16), vc.astype(_BF16), (((1,), (1,)), ((), ())),
                              preferred_element_type=_F32) + p_new * v_new
        stats.append((m, l, num))
    m_all = jnp.maximum(jnp.maximum(stats[0][0], stats[1][0]), stats[2][0])
    num = sum(jnp.exp(m - m_all) * n for m, _, n in stats)
    den = sum(jnp.exp(m - m_all) * l for m, l, _ in stats)
    o = jnp.where(own, num / den, 0.0)
    o_ref[0] = jnp.sum(o, axis=0, keepdims=True)


def _sample_attn(qkv, caches):
    nb = qkv.shape[0]
    in_specs = [pl.BlockSpec((1, 1, 3 * A_WIDTH), lambda i: (i, 0, 0))]
    out_specs = [pl.BlockSpec((1, 1, A_GROUP_WIDTH), lambda i: (i, 0, 0))]
    out_shapes = [jax.ShapeDtypeStruct((nb, 1, A_GROUP_WIDTH), _F32)]
    est = 0
    for c in caches:
        spec = pl.BlockSpec((1,) + c.shape[1:], lambda i: (i, 0, 0))
        in_specs.append(spec)
        out_specs.append(spec)
        out_shapes.append(jax.ShapeDtypeStruct(c.shape, c.dtype))
        est += 6 * c.shape[1] * c.shape[2] * 4
    return pl.pallas_call(
        _sample_attn_kernel,
        grid=(nb,),
        in_specs=in_specs,
        out_specs=out_specs,
        out_shape=out_shapes,
        compiler_params=pltpu.CompilerParams(
            dimension_semantics=("arbitrary",), vmem_limit_bytes=_vmem_limit(est)),
        name="sample_attn",
    )(qkv, *caches)


def _as_column(row_vec):
    n = row_vec.shape[1]
    eye = lax.broadcasted_iota(jnp.int32, (n, n), 0) == lax.broadcasted_iota(jnp.int32, (n, n), 1)
    return jnp.sum(jnp.where(eye, row_vec, 0.0), axis=1, keepdims=True)


def _sample_gla_kernel(q_ref, k_ref, v_ref, gk_ref, r_ref, gn_ref, st_ref, o_ref, new_ref):
    gn = gn_ref[...]
    for h in range(GLA_HEADS):
        dk = slice(h * GLA_HEAD_DK, (h + 1) * GLA_HEAD_DK)
        dv = slice(h * GLA_HEAD_DV, (h + 1) * GLA_HEAD_DV)
        decay = _as_column(jnp.exp(gk_ref[0, :, dk]))
        k_col = _as_column(k_ref[0, :, dk])
        q_col = _as_column(q_ref[0, :, dk])
        state = decay * st_ref[0, h] + k_col * v_ref[0, :, dv]
        new_ref[0, h] = state
        o = jnp.sum(q_col * state, axis=0, keepdims=True)
        rb = r_ref[0, :, dv]
        o_ref[0, :, dv] = _rms_normed(o, gn) * (rb * jax.nn.sigmoid(rb))


def _sample_gla(qb, kb, vb, gk, rb, gn, state):
    nb = qb.shape[0]

    def row_spec(width):
        return pl.BlockSpec((1, 1, width), lambda i: (i, 0, 0))

    state_spec = pl.BlockSpec((1,) + state.shape[1:], lambda i: (i, 0, 0, 0))
    return pl.pallas_call(
        _sample_gla_kernel,
        grid=(nb,),
        in_specs=[row_spec(GLA_DK), row_spec(GLA_DK), row_spec(GLA_DV), row_spec(GLA_DK), row_spec(GLA_DV),
                  pl.BlockSpec((1, GLA_HEAD_DV), lambda i: (0, 0)), state_spec],
        out_specs=[row_spec(GLA_DV), state_spec],
        out_shape=[jax.ShapeDtypeStruct((nb, 1, GLA_DV), _F32), jax.ShapeDtypeStruct(state.shape, state.dtype)],
        compiler_params=pltpu.CompilerParams(dimension_semantics=("arbitrary",)),
        name="sample_gla",
    )(qb, kb, vb, gk, rb, gn, state)


def _merge_ffn_kernel(x_ref, oa_ref, ob_ref, ga_ref, gb_ref, wpa_ref, wpb_ref, wo_ref, g2_ref, wup_ref, wdn_ref,
                      gf_ref, y_ref):
    tm = x_ref.shape[0]
    sub = min(tm, ROW_SUBTILE)
    for r0 in range(0, tm, sub):
        rows = slice(r0, r0 + sub)
        ya = jnp.dot(oa_ref[rows, :].astype(_BF16), wpa_ref[...], preferred_element_type=_F32)
        yb = jnp.dot(ob_ref[rows, :].astype(_BF16), wpb_ref[...], preferred_element_type=_F32)
        mix = (jax.nn.sigmoid(ga_ref[rows, :].astype(_F32)) * ya
               + jax.nn.sigmoid(gb_ref[rows, :].astype(_F32)) * yb)
        h = x_ref[rows, :] + jnp.dot(mix.astype(_BF16), wo_ref[...], preferred_element_type=_F32)
        hn = _rms_normed(h, g2_ref[...]).astype(_BF16)
        acc = h
        for c0 in range(0, D_FF, FF_CHUNK):
            f = jnp.maximum(jnp.dot(hn, wup_ref[:, c0:c0 + FF_CHUNK], preferred_element_type=_F32), 0.0)
            acc = acc + jnp.dot((f * f).astype(_BF16), wdn_ref[c0:c0 + FF_CHUNK, :], preferred_element_type=_F32)
        y_ref[rows, :] = _rms_normed(acc, gf_ref[...])


def _merge_ffn(x, oa, ob, ga, gb, wpa, wpb, wo, g2, wup, wdn, gf, tm):
    m, d = x.shape

    def rows(width):
        return pl.BlockSpec((tm, width), lambda i: (i, 0))

    weights = (wpa, wpb, wo, wup, wdn)
    est = (sum(w.size for w in weights) * 2 + 2 * tm * (2 * d * 4 + A_GROUP_WIDTH * 4 + 3 * d * 4)
           + ROW_SUBTILE * (6 * d + 2 * FF_CHUNK) * 4)
    return pl.pallas_call(
        _merge_ffn_kernel,
        grid=(m // tm,),
        in_specs=[rows(d), rows(A_GROUP_WIDTH), rows(GLA_DV), rows(d), rows(d),
                  _resident(wpa.shape), _resident(wpb.shape), _resident(wo.shape), _resident((1, d)),
                  _resident(wup.shape), _resident(wdn.shape), _resident((1, d))],
        out_specs=rows(d),
        out_shape=jax.ShapeDtypeStruct((m, d), _F32),
        compiler_params=pltpu.CompilerParams(
            dimension_semantics=("arbitrary",), vmem_limit_bytes=_vmem_limit(est)),
        name="merge_ffn",
    )(x, oa, ob, ga, gb, wpa, wpb, wo, g2, wup, wdn, gf)


def _split_w_in(w_in):
    sizes = (A_WIDTH, A_WIDTH, A_WIDTH, GLA_DK, GLA_DK, GLA_DV, GLA_DV, GLA_GATE_RANK, D_MODEL, D_MODEL)
    offs = np.concatenate([[0], np.cumsum(sizes)])
    qa, ka, va, qb, kb, vb, rb, glr, ga, gb = (w_in[:, offs[j]:offs[j + 1]] for j in range(len(sizes)))
    glr = jnp.pad(glr, ((0, 0), (0, LANES - GLA_GATE_RANK)))
    w_attn = jnp.concatenate([qa, ka, va], axis=1).astype(_BF16)
    w_gla = jnp.concatenate([qb, kb, vb, rb, ga, gb, glr], axis=1).astype(_BF16)
    return w_attn, w_gla


def kernel(x_prompt, x_sample, cache_a1_k, cache_a1_v, cache_a2_k, cache_a2_v, cache_a3_k, cache_a3_v,
           state_gla, g_norm1, w_in, w_gk2, b_gk, g_gla, w_pa, w_pb, w_o, g_norm2, w_up, w_down, g_final):
    assert g_norm1.shape[0] == 1, "one layer"
    b, t, d = x_prompt.shape
    nb = x_sample.shape[0]
    assert x_sample.shape[1] == 1, "one new token per sample row"

    w_attn, w_gla = _split_w_in(w_in[0])
    w_gk = jnp.pad(w_gk2[0], ((0, LANES - GLA_GATE_RANK), (0, 0))).astype(_BF16)
    b_gk_row = b_gk[0][None, :]
    g1 = g_norm1[0][None, :]
    g2 = g_norm2[0][None, :]
    gf = g_final[None, :]
    gn = g_gla[0][None, :]
    wpa = w_pa[0].astype(_BF16)
    wpb = w_pb[0].astype(_BF16)
    wo = w_o[0].astype(_BF16)
    wup = w_up[0].astype(_BF16)
    wdn = w_down[0].astype(_BF16)

    cos_p, sin_p = _rope_tables(t, 0, 1)
    outs = _attn_proj_prompt(x_prompt, g1, w_attn, cos_p, sin_p)
    up, packs, p_caches = outs[0], outs[1:4], outs[4:]
    xp = x_prompt.reshape(b * t, d)
    qb, kb, vb, rb, ga, gb, gk = _gla_proj(up.reshape(b * t, d), g1, w_gla, w_gk, b_gk_row, ROW_TILE, _BF16)
    oa = _dilated_attn(packs, t)

    def seq(z):
        return z.reshape(b, t, z.shape[-1])

    ob, p_gla = _gla_chunked(seq(qb), seq(kb), seq(vb), seq(gk), seq(rb), gn)
    y_prompt = _merge_ffn(xp, oa.reshape(b * t, A_GROUP_WIDTH), ob.reshape(b * t, GLA_DV), ga, gb,
                          wpa, wpb, wo, g2, wup, wdn, gf, ROW_TILE).reshape(b, t, d)

    cos_s, sin_s = _rope_tables(nb, PAST_LEN, 0)
    qkv_s = _attn_proj_sample(x_sample.reshape(1, nb, d), g1, w_attn, cos_s, sin_s).reshape(nb, 1, 3 * A_WIDTH)
    xs = x_sample.reshape(nb, d)
    qb_s, kb_s, vb_s, rb_s, ga_s, gb_s, gk_s = _gla_proj(xs, g1, w_gla, w_gk, b_gk_row, nb, _F32)
    caches = [jnp.transpose(c[0], (0, 2, 3, 1)).reshape(nb, A_GROUP_WIDTH, c.shape[2])
              for c in (cache_a1_k, cache_a1_v, cache_a2_k, cache_a2_v, cache_a3_k, cache_a3_v)]
    s_outs = _sample_attn(qkv_s, caches)
    oa_s, s_caches = s_outs[0], s_outs[1:]

    def tok(z):
        return z.reshape(nb, 1, z.shape[-1])

    ob_s, s_gla = _sample_gla(tok(qb_s), tok(kb_s), tok(vb_s), tok(gk_s), tok(rb_s), gn, state_gla[0])
    y_sample = _merge_ffn(xs, oa_s.reshape(nb, A_GROUP_WIDTH), ob_s.reshape(nb, GLA_DV), ga_s, gb_s,
                          wpa, wpb, wo, g2, wup, wdn, gf, nb).reshape(nb, 1, d)

    def prompt_cache_out(z):
        return z.reshape(1, z.shape[0], z.shape[1], A_HEADS_PER_GROUP, A_HEAD_DIM)

    def sample_cache_out(z):
        z = z.reshape(z.shape[0], A_HEADS_PER_GROUP, A_HEAD_DIM, z.shape[2])
        return jnp.transpose(z, (0, 3, 1, 2))[None]

    return (y_prompt, y_sample, *[prompt_cache_out(z) for z in p_caches], p_gla[None],
            *[sample_cache_out(z) for z in s_caches], s_gla[None])
```

```python
import functools

import jax
import jax.numpy as jnp
from jax import lax
from jax.experimental import pallas as pl
from jax.experimental.pallas import tpu as pltpu

D_MODEL = 1024
PAST_LEN = 16384
A_GROUPS = ((128, 1), (512, 4), (2048, 16))
A_N_GROUPS = 3
A_HEADS_PER_GROUP = 4
A_HEAD_DIM = 64
A_GROUP_WIDTH = A_HEADS_PER_GROUP * A_HEAD_DIM
A_WIDTH = A_N_GROUPS * A_GROUP_WIDTH
A_WINDOW_KEYS = 128
ROPE_THETA = 10000.0
GLA_HEADS = 4
GLA_HEAD_DK = 128
GLA_HEAD_DV = 256
GLA_DK = GLA_HEADS * GLA_HEAD_DK
GLA_DV = GLA_HEADS * GLA_HEAD_DV
GLA_GATE_RANK = 16
GLA_GATE_NORM = 16.0
D_FF = 4 * D_MODEL
EPS = 1e-6
LOG2_E = 1.4426950408889634

LANES = 128
V7X_SCOPED_VMEM_CAP = 56 * 1024 * 1024

ROW_TILE = 512
ROW_SUBTILE = 256
Q_BLOCK = 128
ATTN_BLOCKS_PER_STEP = 16
GLA_CHUNK = 128
GLA_ROW_TILE = 512
GLA_FACTOR_RANGE = 40.0
FF_CHUNK = 1024

_F32 = jnp.float32
_BF16 = jnp.bfloat16
_NEG_INF = float("-inf")


def _vmem_limit(nbytes):
    return int(min(V7X_SCOPED_VMEM_CAP, max(16 * 1024 * 1024, nbytes * 3 // 2)))


def _params(n_grid_dims, est_bytes):
    return pltpu.CompilerParams(
        dimension_semantics=("arbitrary",) * n_grid_dims, vmem_limit_bytes=_vmem_limit(est_bytes))


def _resident(shape):
    nd = len(shape)
    return pl.BlockSpec(shape, lambda *_: (0,) * nd, pipeline_mode=pl.Buffered(1))


def _rms_normed(x, g):
    return x * lax.rsqrt(jnp.mean(x * x, axis=-1, keepdims=True) + EPS) * g


def _rope_table_kernel(inv_ref, sign_ref, cos_ref, sin_ref, *, pos0, pos_step):
    rows = cos_ref.shape[0]
    row = lax.broadcasted_iota(jnp.int32, (rows, LANES), 0) + pl.program_id(0) * rows
    pos = (pos0 + row * pos_step).astype(_F32)
    ang = pos * inv_ref[...]
    cos_ref[...] = jnp.cos(ang)
    sin_ref[...] = jnp.sin(ang) * sign_ref[...]


def _rope_tables(n_rows, pos0, pos_step):
    half = A_HEAD_DIM // 2
    lane = jnp.arange(LANES)
    inv = ROPE_THETA ** (-((lane % half).astype(_F32)) / half)
    sign = jnp.where((lane % A_HEAD_DIM) < half, -1.0, 1.0).astype(_F32)
    tile = min(n_rows, ROW_TILE)
    return pl.pallas_call(
        functools.partial(_rope_table_kernel, pos0=pos0, pos_step=pos_step),
        grid=(n_rows // tile,),
        in_specs=[pl.BlockSpec((1, LANES), lambda i: (0, 0))] * 2,
        out_specs=[pl.BlockSpec((tile, LANES), lambda i: (i, 0))] * 2,
        out_shape=[jax.ShapeDtypeStruct((n_rows, LANES), _F32)] * 2,
        name="rope_table",
    )(inv[None, :], sign[None, :])


def _rope_slab(z, cos, sin_signed):
    half = A_HEAD_DIM // 2
    lane = lax.broadcasted_iota(jnp.int32, z.shape, 1)
    partner = jnp.where((lane % A_HEAD_DIM) < half, pltpu.roll(z, LANES - half, 1), pltpu.roll(z, half, 1))
    return z * cos + partner * sin_signed


_N_SLABS = 3 * A_WIDTH // LANES
_SLABS_PER_SECTION = A_WIDTH // LANES
_SLABS_PER_GROUP = A_GROUP_WIDTH // LANES


def _attn_proj_slabs(u, w_ref, cos, sin, q_scale):
    y = jnp.dot(u, w_ref[...], preferred_element_type=_F32)
    for s in range(_N_SLABS):
        z = y[:, s * LANES:(s + 1) * LANES]
        if s < _SLABS_PER_SECTION:
            z = _rope_slab(z, cos, sin) * q_scale
        elif s < 2 * _SLABS_PER_SECTION:
            z = _rope_slab(z, cos, sin)
        yield s, z


def _attn_proj_prompt_kernel(x_ref, g_ref, w_ref, cos_ref, sin_ref,
                             u_ref, p1_ref, p2_ref, p3_ref, c1k_ref, c1v_ref, c2k_ref, c2v_ref, c3k_ref, c3v_ref,
                             slab_ref, *, tm, n_tiles):
    i = pl.program_id(1)
    packs = (p1_ref, p2_ref, p3_ref)
    caches = ((c1k_ref, c1v_ref), (c2k_ref, c2v_ref), (c3k_ref, c3v_ref))
    sub = min(tm, ROW_SUBTILE)
    for r0 in range(0, tm, sub):
        rows = slice(r0, r0 + sub)
        u_ref[0, rows, :] = _rms_normed(x_ref[0, rows, :], g_ref[...]).astype(_BF16)
        for s, z in _attn_proj_slabs(u_ref[0, rows, :], w_ref, cos_ref[rows, :], sin_ref[rows, :], A_HEAD_DIM ** -0.5 * LOG2_E):
            slab_ref[s, rows, :] = z
            section, within = divmod(s, _SLABS_PER_SECTION)
            group, pair = divmod(within, _SLABS_PER_GROUP)
            dil = A_GROUPS[group][1]
            col = (section * _SLABS_PER_GROUP + pair) * LANES
            for r in range(dil):
                picked = slab_ref[s, pl.ds(r0 + r, sub // dil, stride=dil), :] if dil > 1 else z
                packs[group][0, r, r0 // dil:(r0 + sub) // dil, col:col + LANES] = picked.astype(_BF16)

    for group, (window, _) in enumerate(A_GROUPS):
        keep = min(window, tm)
        first_kept_tile = n_tiles - max(window // tm, 1)

        @pl.when(i >= first_kept_tile)
        def _(group=group, keep=keep):
            for section in (1, 2):
                for pair in range(_SLABS_PER_GROUP):
                    s = section * _SLABS_PER_SECTION + group * _SLABS_PER_GROUP + pair
                    caches[group][section - 1][0, :, pair * LANES:(pair + 1) * LANES] = slab_ref[s, tm - keep:tm, :]


def _attn_proj_sample_kernel(x_ref, g_ref, w_ref, cos_ref, sin_ref, o_ref):
    u = _rms_normed(x_ref[0], g_ref[...]).astype(_BF16)
    for s, z in _attn_proj_slabs(u, w_ref, cos_ref[...], sin_ref[...], A_HEAD_DIM ** -0.5):
        o_ref[0, :, s * LANES:(s + 1) * LANES] = z


def _attn_proj_prompt(x, g, w, cos, sin):
    b, t, d = x.shape
    tm = ROW_TILE
    n_tiles = t // tm
    pack_shapes, pack_specs = [], []
    for _, dil in A_GROUPS:
        pack_shapes.append(jax.ShapeDtypeStruct((b, dil, t // dil, 3 * A_GROUP_WIDTH), _BF16))
        pack_specs.append(pl.BlockSpec((1, dil, tm // dil, 3 * A_GROUP_WIDTH), lambda bi, i: (bi, 0, i, 0)))
    cache_shapes, cache_specs = [], []
    for window, _ in A_GROUPS:
        keep = min(window, t)
        blk = min(keep, tm)
        first = n_tiles - max(keep // tm, 1)
        for _ in range(2):
            cache_shapes.append(jax.ShapeDtypeStruct((b, keep, A_GROUP_WIDTH), _F32))
            cache_specs.append(pl.BlockSpec(
                (1, blk, A_GROUP_WIDTH), lambda bi, i, first=first: (bi, jnp.maximum(i - first, 0), 0)))
    est = (d * 3 * A_WIDTH * 2 + 2 * tm * d * 4 + 2 * 3 * tm * 3 * A_GROUP_WIDTH * 2
           + 2 * 6 * tm * A_GROUP_WIDTH * 4 + 2 * tm * 3 * A_WIDTH * 4)
    return pl.pallas_call(
        functools.partial(_attn_proj_prompt_kernel, tm=tm, n_tiles=n_tiles),
        grid=(b, n_tiles),
        in_specs=[
            pl.BlockSpec((1, tm, d), lambda bi, i: (bi, i, 0)),
            _resident((1, d)),
            _resident((d, 3 * A_WIDTH)),
            pl.BlockSpec((tm, LANES), lambda bi, i: (i, 0)),
            pl.BlockSpec((tm, LANES), lambda bi, i: (i, 0)),
        ],
        out_specs=[pl.BlockSpec((1, tm, d), lambda bi, i: (bi, i, 0))] + pack_specs + cache_specs,
        out_shape=[jax.ShapeDtypeStruct((b, t, d), _BF16)] + pack_shapes + cache_shapes,
        scratch_shapes=[pltpu.VMEM((_N_SLABS, tm, LANES), _F32)],
        compiler_params=_params(2, est),
        name="attn_proj_prompt",
    )(x, g, w, cos, sin)


def _attn_proj_sample(x, g, w, cos, sin):
    _, m, d = x.shape
    est = d * 3 * A_WIDTH * 2 * 2 + 4 * m * 3 * A_WIDTH * 4
    return pl.pallas_call(
        _attn_proj_sample_kernel,
        grid=(1,),
        in_specs=[
            pl.BlockSpec((1, m, d), lambda i: (0, 0, 0)),
            pl.BlockSpec((1, d), lambda i: (0, 0)),
            pl.BlockSpec((d, 3 * A_WIDTH), lambda i: (0, 0)),
            pl.BlockSpec((m, LANES), lambda i: (0, 0)),
            pl.BlockSpec((m, LANES), lambda i: (0, 0)),
        ],
        out_specs=pl.BlockSpec((1, m, 3 * A_WIDTH), lambda i: (0, 0, 0)),
        out_shape=jax.ShapeDtypeStruct((1, m, 3 * A_WIDTH), _F32),
        compiler_params=pltpu.CompilerParams(vmem_limit_bytes=_vmem_limit(est)),
        name="attn_proj_sample",
    )(x, g, w, cos, sin)


_GP_SECTIONS = {}
_off = 3 * A_WIDTH
for _name, _width in (("qb", GLA_DK), ("kb", GLA_DK), ("vb", GLA_DV), ("rb", GLA_DV), ("glr", LANES)):
    _GP_SECTIONS[_name] = (_off, _width)
    _off += _width
_GATES_OFFSET = 3 * A_WIDTH + 2 * GLA_DK + 2 * GLA_DV + GLA_GATE_RANK


def _gla_project(u, w_ref, wg_ref, wgk_ref, bgk_ref):
    def section(name):
        off, width = _GP_SECTIONS[name]
        return jnp.dot(u, w_ref[:, off:off + width], preferred_element_type=_F32)

    glr = section("glr").astype(_BF16)
    z = jnp.dot(glr, wgk_ref[...], preferred_element_type=_F32) + bgk_ref[...]
    log_sigmoid = jnp.minimum(z, 0.0) - jnp.log(1.0 + jnp.exp(-jnp.abs(z)))
    yield "gk", log_sigmoid / GLA_GATE_NORM
    yield "qb", section("qb") * (GLA_HEAD_DK ** -0.5)
    for name in ("kb", "vb", "rb"):
        yield name, section(name)
    yield "ga", jnp.dot(u, wg_ref[:, :D_MODEL], preferred_element_type=_F32)
    yield "gb", jnp.dot(u, wg_ref[:, D_MODEL:], preferred_element_type=_F32)


def _gla_proj_kernel(x_ref, g_ref, w_ref, wg_ref, wgk_ref, bgk_ref,
                     qb_ref, kb_ref, vb_ref, rb_ref, ga_ref, gb_ref, gk_ref):
    outs = dict(qb=qb_ref, kb=kb_ref, vb=vb_ref, rb=rb_ref, ga=ga_ref, gb=gb_ref, gk=gk_ref)
    u = _rms_normed(x_ref[...], g_ref[...]).astype(_BF16)
    for name, y in _gla_project(u, w_ref, wg_ref, wgk_ref, bgk_ref):
        outs[name][...] = y.astype(outs[name].dtype)


def _gla_proj(x, g, w, w_gates, wgk, bgk):
    m, d = x.shape
    widths = (GLA_DK, GLA_DK, GLA_DV, GLA_DV, D_MODEL, D_MODEL, GLA_DK)
    est = (w.size + w_gates.size) * 2 + 4 * m * (d + sum(widths)) * 4

    def whole(shape):
        return pl.BlockSpec(shape, lambda i: (0,) * len(shape))

    return pl.pallas_call(
        _gla_proj_kernel,
        grid=(1,),
        in_specs=[whole(x.shape), whole(g.shape), _resident(w.shape), _resident(w_gates.shape), whole(wgk.shape),
                  whole(bgk.shape)],
        out_specs=[whole((m, wd)) for wd in widths],
        out_shape=[jax.ShapeDtypeStruct((m, wd), _F32) for wd in widths],
        compiler_params=_params(1, est),
        name="gla_proj",
    )(x, g, w, w_gates, wgk, bgk)


def _window_attention(q, k, v, bias, head0):
    qb = Q_BLOCK
    zero = jnp.zeros_like(q)
    q_heads = jnp.concatenate([jnp.where(head0, q, zero), jnp.where(head0, zero, q)], axis=0)
    s = lax.dot_general(q_heads, k, (((1,), (1,)), ((), ())), preferred_element_type=_F32)
    s = s + jnp.concatenate([bias, bias], axis=0)
    m = jnp.max(s, axis=1, keepdims=True)
    p = jnp.exp2(s - m)
    l = jnp.sum(p, axis=1, keepdims=True)
    num = jnp.dot(p.astype(_BF16), v, preferred_element_type=_F32)
    return (jnp.where(head0, m[:qb], m[qb:]), jnp.where(head0, l[:qb], l[qb:]),
            jnp.where(head0, num[:qb], num[qb:]))


def _dilated_attn_kernel(q1_ref, k1_ref, v1_ref, q2_ref, k2_ref, v2_ref, q3_ref, k3_ref, v3_ref,
                         o_ref, m_ref, l_ref, n_ref, bias_ref):
    qb = Q_BLOCK
    lane = lax.broadcasted_iota(jnp.int32, (qb, LANES), 1)
    head0 = lane < A_HEAD_DIM
    row = lax.broadcasted_iota(jnp.int32, (qb, 2 * qb), 0)
    col = lax.broadcasted_iota(jnp.int32, (qb, 2 * qb), 1)
    for j, offset in enumerate((0, qb)):
        diff = row - col + offset
        bias_ref[j] = jnp.where((diff >= 0) & (diff <= A_WINDOW_KEYS), 0.0, _NEG_INF)
    groups = ((q1_ref, k1_ref, v1_ref), (q2_ref, k2_ref, v2_ref), (q3_ref, k3_ref, v3_ref))
    first_group, last_group = A_N_GROUPS - 1, 0
    per_step = ATTN_BLOCKS_PER_STEP

    for group in range(A_N_GROUPS - 1, -1, -1):
        q_ref, k_ref, v_ref = groups[group]
        dil = A_GROUPS[group][1]
        n_blocks = q_ref.shape[2] // qb
        res_per_step = min(dil, per_step)
        blk_per_step = per_step // res_per_step
        blk_steps = n_blocks // blk_per_step

        def step(idx, carry, q_ref=q_ref, k_ref=k_ref, v_ref=v_ref, dil=dil, group=group,
                 res_per_step=res_per_step, blk_per_step=blk_per_step, blk_steps=blk_steps):
            stats, rows = [], []
            for u in range(per_step):
                if dil == 1:
                    r = 0
                else:
                    r = (idx // blk_steps) * res_per_step + u % res_per_step
                n = (idx % blk_steps) * blk_per_step + u // res_per_step
                q0 = pl.multiple_of(n * qb, qb)
                k0 = pl.multiple_of(jnp.maximum(n - 1, 0) * qb, qb)
                q = q_ref[0, r, pl.ds(q0, qb), :]
                k = k_ref[0, r, pl.ds(k0, 2 * qb), :]
                v = v_ref[0, r, pl.ds(k0, 2 * qb), :]
                stats.append(_window_attention(q, k, v, bias_ref[jnp.minimum(n, 1)], head0))
                rows.append(pl.ds(q0, qb) if dil == 1 else pl.ds(q0 * dil + r, qb, stride=dil))
            if group == first_group:
                for (m_g, l_g, n_g), rw in zip(stats, rows):
                    m_ref[rw, :] = m_g
                    l_ref[rw, :] = l_g
                    n_ref[rw, :] = n_g
                return carry
            old = [(m_ref[rw, :], l_ref[rw, :], n_ref[rw, :]) for rw in rows]
            for (m_g, l_g, n_g), (m_old, l_old, n_old), rw in zip(stats, old, rows):
                m_new = jnp.maximum(m_old, m_g)
                a = jnp.exp2(m_old - m_new)
                b = jnp.exp2(m_g - m_new)
                l_new = a * l_old + b * l_g
                n_new = a * n_old + b * n_g
                if group == last_group:
                    o_ref[0, rw, :] = (n_new / l_new).astype(o_ref.dtype)
                else:
                    m_ref[rw, :] = m_new
                    l_ref[rw, :] = l_new
                    n_ref[rw, :] = n_new
            return carry

        lax.fori_loop(0, dil * n_blocks // per_step, step, 0)


def _dilated_attn(packs, t):
    b = packs[0].shape[0]
    n_pairs = A_GROUP_WIDTH // LANES
    in_specs, operands = [], []
    for pack, (_, dil) in zip(packs, A_GROUPS):
        for section in range(3):
            in_specs.append(pl.BlockSpec(
                (1, dil, t // dil, LANES), lambda bi, p, section=section: (bi, 0, 0, section * n_pairs + p)))
            operands.append(pack)
    est = 2 * 9 * t * LANES * 2 + 2 * t * LANES * 2 + 3 * t * LANES * 4
    return pl.pallas_call(
        _dilated_attn_kernel,
        grid=(b, n_pairs),
        in_specs=in_specs,
        out_specs=pl.BlockSpec((1, t, LANES), lambda bi, p: (bi, 0, p)),
        out_shape=jax.ShapeDtypeStruct((b, t, A_GROUP_WIDTH), _BF16),
        scratch_shapes=[pltpu.VMEM((t, LANES), _F32)] * 3 + [pltpu.VMEM((2, Q_BLOCK, 2 * Q_BLOCK), _F32)],
        compiler_params=_params(2, est),
        name="dilated_attn",
    )(*operands)


def _pairwise_decay_scores(q, k, cum):
    c = q.shape[0]
    row = lax.broadcasted_iota(jnp.int32, (c, c), 0)
    col = lax.broadcasted_iota(jnp.int32, (c, c), 1)
    row_d = lax.broadcasted_iota(jnp.int32, q.shape, 0)

    def sub_diagonal(delta, acc):
        k_back = pltpu.roll(k, delta, 0)
        cum_back = pltpu.roll(cum, delta, 0)
        log_decay = jnp.where(row_d >= delta, cum - cum_back, 0.0)
        diag = jnp.sum(q * k_back * jnp.exp2(log_decay), axis=1, keepdims=True)
        return acc + jnp.where(row - col == delta, diag, 0.0)

    return lax.fori_loop(0, c, sub_diagonal, jnp.zeros((c, c), _F32))


def _gla_tile(q_ref, k_ref, v_ref, gk_ref, r_ref, gn, o_ref, st_ref, cum_ref, sc_ref, interleaved=()):
    c = GLA_CHUNK
    n_chunks = q_ref.shape[0] // c
    row = lax.broadcasted_iota(jnp.int32, (c, c), 0)
    col = lax.broadcasted_iota(jnp.int32, (c, c), 1)
    causal = row >= col
    causal_bf = jnp.where(causal, 1.0, 0.0).astype(_BF16)

    def head_cols(h):
        return (slice(h * GLA_HEAD_DK, (h + 1) * GLA_HEAD_DK), slice(h * GLA_HEAD_DV, (h + 1) * GLA_HEAD_DV))

    spread = jnp.zeros((1, GLA_DK), _F32)
    for ci in range(n_chunks):
        rows = slice(ci * c, (ci + 1) * c)
        g = gk_ref[rows, :]
        g_hi = g.astype(_BF16)
        g_lo = (g - g_hi.astype(_F32)).astype(_BF16)
        parts = jnp.dot(causal_bf, jnp.concatenate([g_hi, g_lo], axis=1), preferred_element_type=_F32)
        cum = (parts[:, :GLA_DK] + parts[:, GLA_DK:]) * LOG2_E
        cum_ref[rows, :] = cum
        mid = cum[c // 2 - 1:c // 2, :]
        spread = jnp.maximum(spread, jnp.maximum(cum[0:1, :] - mid, mid - cum[c - 1:c, :]))
    factorable = jnp.max(spread) <= GLA_FACTOR_RANGE * LOG2_E

    @pl.when(factorable)
    def _():
        for ci in range(n_chunks):
            rows = slice(ci * c, (ci + 1) * c)
            for h in range(GLA_HEADS):
                dk, _ = head_cols(h)
                cum = cum_ref[rows, dk]
                mid = cum[c // 2 - 1:c // 2, :]
                q_mid = (q_ref[rows, dk].astype(_F32) * jnp.exp2(cum - mid)).astype(_BF16)
                k_mid = (k_ref[rows, dk].astype(_F32) * jnp.exp2(mid - cum)).astype(_BF16)
                scores = lax.dot_general(q_mid, k_mid, (((1,), (1,)), ((), ())), preferred_element_type=_F32)
                sc_ref[ci * GLA_HEADS + h] = jnp.where(causal, scores, 0.0).astype(_BF16)

    @pl.when(jnp.logical_not(factorable))
    def _():
        for ci in range(n_chunks):
            rows = slice(ci * c, (ci + 1) * c)
            for h in range(GLA_HEADS):
                dk, _ = head_cols(h)
                scores = _pairwise_decay_scores(q_ref[rows, dk].astype(_F32), k_ref[rows, dk].astype(_F32),
                                                cum_ref[rows, dk])
                sc_ref[ci * GLA_HEADS + h] = scores.astype(_BF16)

    interleaved = list(interleaved)
    states = [st_ref[h] for h in range(GLA_HEADS)]
    for ci in range(n_chunks):
        rows = slice(ci * c, (ci + 1) * c)
        for h in range(GLA_HEADS):
            dk, dv = head_cols(h)
            cum = cum_ref[rows, dk]
            last = cum[c - 1:c, :]
            v = v_ref[rows, dv]
            q_in = (q_ref[rows, dk].astype(_F32) * jnp.exp2(cum)).astype(_BF16)
            k_out = (k_ref[rows, dk].astype(_F32) * jnp.exp2(last - cum)).astype(_BF16)
            o = jnp.dot(sc_ref[ci * GLA_HEADS + h], v, preferred_element_type=_F32)
            o = o + lax.dot_general(q_in, states[h].astype(_BF16), (((1,), (1,)), ((), ())),
                                    preferred_element_type=_F32)
            update_t = lax.dot_general(v, k_out, (((0,), (0,)), ((), ())), preferred_element_type=_F32)
            states[h] = states[h] * jnp.exp2(last) + update_t
            rb = r_ref[rows, dv].astype(_F32)
            o = _rms_normed(o, gn) * (rb * jax.nn.sigmoid(rb))
            o_ref[rows, dv] = o.astype(o_ref.dtype)
        if ci < len(interleaved):
            interleaved[ci]()
    for work in interleaved[n_chunks:]:
        work()
    for h in range(GLA_HEADS):
        st_ref[h] = states[h]


def _gla_fused_kernel(u_ref, w_ref, wg_ref, wgk_ref, bgk_ref, gn_ref,
                      ga_ref, gb_ref, o_ref, fin_ref,
                      qs_ref, ks_ref, vs_ref, rs_ref, gks_ref, st_ref, cum_ref, sc_ref, *, tiles_per_seq):
    j = pl.program_id(0)
    tm = u_ref.shape[0]
    write_half = j % 2
    read_half = 1 - write_half
    gla_tile = jnp.maximum(j - 1, 0)
    projected = dict(qb=qs_ref, kb=ks_ref, vb=vs_ref, rb=rs_ref, gk=gks_ref)

    @pl.when(j == 0)
    def _():
        for ref in projected.values():
            ref[...] = jnp.zeros_like(ref)

    @pl.when(gla_tile % tiles_per_seq == 0)
    def _():
        st_ref[...] = jnp.zeros_like(st_ref)

    def project(r0, sub):
        def run():
            rows = slice(r0, r0 + sub)
            for name, y in _gla_project(u_ref[rows, :], w_ref, wg_ref, wgk_ref, bgk_ref):
                if name == "ga":
                    ga_ref[rows, :] = y.astype(ga_ref.dtype)
                elif name == "gb":
                    gb_ref[rows, :] = y.astype(gb_ref.dtype)
                else:
                    projected[name][write_half, rows, :] = y.astype(projected[name].dtype)
        return run

    sub = min(tm, ROW_SUBTILE)
    _gla_tile(qs_ref.at[read_half], ks_ref.at[read_half], vs_ref.at[read_half], gks_ref.at[read_half],
              rs_ref.at[read_half], gn_ref[...], o_ref, st_ref, cum_ref, sc_ref,
              interleaved=[project(r0, sub) for r0 in range(0, tm, sub)])

    @pl.when((gla_tile % tiles_per_seq == tiles_per_seq - 1) & (j >= 1))
    def _():
        for h in range(GLA_HEADS):
            fin_ref[0, h] = st_ref[h].T


def _gla_fused(u, w, w_gates, wgk, bgk, gn, tiles_per_seq):
    m, d = u.shape
    tm = GLA_ROW_TILE
    n_tiles = m // tm
    n_seq = n_tiles // tiles_per_seq

    def proj_tile(width):
        return pl.BlockSpec((tm, width), lambda j: (jnp.minimum(j, n_tiles - 1), 0))

    def gla_tile(width):
        return pl.BlockSpec((tm, width), lambda j: (jnp.maximum(j - 1, 0), 0))

    est = ((w.size + w_gates.size) * 2 + 2 * tm * (d * 2 + 2 * D_MODEL * 2 + GLA_DV * 2)
           + 2 * tm * (2 * GLA_DK * 2 + 2 * GLA_DV * 2 + GLA_DK * 4) + tm * GLA_DK * 4
           + 4 * GLA_HEADS * GLA_HEAD_DK * GLA_HEAD_DV * 4 + 2 * ROW_SUBTILE * GLA_DV * 4)
    return pl.pallas_call(
        functools.partial(_gla_fused_kernel, tiles_per_seq=tiles_per_seq),
        grid=(n_tiles + 1,),
        in_specs=[proj_tile(d), _resident(w.shape), _resident(w_gates.shape), _resident(wgk.shape),
                  _resident(bgk.shape), _resident(gn.shape)],
        out_specs=[
            proj_tile(D_MODEL), proj_tile(D_MODEL), gla_tile(GLA_DV),
            pl.BlockSpec((1, GLA_HEADS, GLA_HEAD_DK, GLA_HEAD_DV),
                         lambda j: (jnp.maximum(j - 1, 0) // tiles_per_seq, 0, 0, 0)),
        ],
        out_shape=[
            jax.ShapeDtypeStruct((m, D_MODEL), _BF16), jax.ShapeDtypeStruct((m, D_MODEL), _BF16),
            jax.ShapeDtypeStruct((m, GLA_DV), _BF16),
            jax.ShapeDtypeStruct((n_seq, GLA_HEADS, GLA_HEAD_DK, GLA_HEAD_DV), _F32),
        ],
        scratch_shapes=[
            pltpu.VMEM((2, tm, GLA_DK), _BF16), pltpu.VMEM((2, tm, GLA_DK), _BF16),
            pltpu.VMEM((2, tm, GLA_DV), _BF16), pltpu.VMEM((2, tm, GLA_DV), _BF16),
            pltpu.VMEM((2, tm, GLA_DK), _F32),
            pltpu.VMEM((GLA_HEADS, GLA_HEAD_DV, GLA_HEAD_DK), _F32),
            pltpu.VMEM((tm, GLA_DK), _F32),
            pltpu.VMEM((tm // GLA_CHUNK * GLA_HEADS, GLA_CHUNK, GLA_CHUNK), _BF16),
        ],
        compiler_params=_params(1, est),
        name="gla_fused",
    )(u, w, w_gates, wgk, bgk, gn)


def _sample_attn_kernel(qkv_ref, c1k_ref, c1v_ref, c2k_ref, c2v_ref, c3k_ref, c3v_ref,
                        o_ref, n1k_ref, n1v_ref, n2k_ref, n2v_ref, n3k_ref, n3v_ref):
    caches = ((c1k_ref, c1v_ref, n1k_ref, n1v_ref), (c2k_ref, c2v_ref, n2k_ref, n2v_ref),
              (c3k_ref, c3v_ref, n3k_ref, n3v_ref))
    head_rows = 8
    hrow = lax.broadcasted_iota(jnp.int32, (head_rows, A_GROUP_WIDTH), 0)
    hlane = lax.broadcasted_iota(jnp.int32, (head_rows, A_GROUP_WIDTH), 1)
    own = (hlane // A_HEAD_DIM) == hrow
    stats = []
    for group, (kc_ref, vc_ref, kn_ref, vn_ref) in enumerate(caches):
        window, dil = A_GROUPS[group]
        base = group * A_GROUP_WIDTH
        q = qkv_ref[0, :, base:base + A_GROUP_WIDTH]
        k_new = qkv_ref[0, :, A_WIDTH + base:A_WIDTH + base + A_GROUP_WIDTH]
        v_new = qkv_ref[0, :, 2 * A_WIDTH + base:2 * A_WIDTH + base + A_GROUP_WIDTH]
        kc = kc_ref[0]
        vc = vc_ref[0]
        is_last = lax.broadcasted_iota(jnp.int32, (A_GROUP_WIDTH, window), 1) == window - 1
        kn_ref[0] = jnp.where(is_last, _as_column(k_new), pltpu.roll(kc, window - 1, 1))
        vn_ref[0] = jnp.where(is_last, _as_column(v_new), pltpu.roll(vc, window - 1, 1))

        q_heads = jnp.where(own, q, 0.0)
        s_old = jnp.dot(q_heads.astype(_BF16), kc.astype(_BF16), preferred_element_type=_F32)
        pos = lax.broadcasted_iota(jnp.int32, (head_rows, window), 1)
        s_old = jnp.where(pos % dil == 0, s_old, _NEG_INF)
        s_new = jnp.sum(q_heads * k_new, axis=1, keepdims=True)
        m = jnp.maximum(jnp.max(s_old, axis=1, keepdims=True), s_new)
        p_old = jnp.exp(s_old - m)
        p_new = jnp.exp(s_new - m)
        l = jnp.sum(p_old, axis=1, keepdims=True) + p_new
        num = lax.dot_general(p_old.astype(_BF16), vc.astype(_BF16), (((1,), (1,)), ((), ())),
                              preferred_element_type=_F32) + p_new * v_new
        stats.append((m, l, num))
    m_all = jnp.maximum(jnp.maximum(stats[0][0], stats[1][0]), stats[2][0])
    num = sum(jnp.exp(m - m_all) * n for m, _, n in stats)
    den = sum(jnp.exp(m - m_all) * l for m, l, _ in stats)
    o = jnp.where(own, num / den, 0.0)
    o_ref[0] = jnp.sum(o, axis=0, keepdims=True)


def _sample_attn(qkv, caches):
    nb = qkv.shape[0]
    in_specs = [pl.BlockSpec((1, 1, 3 * A_WIDTH), lambda i: (i, 0, 0))]
    out_specs = [pl.BlockSpec((1, 1, A_GROUP_WIDTH), lambda i: (i, 0, 0))]
    out_shapes = [jax.ShapeDtypeStruct((nb, 1, A_GROUP_WIDTH), _F32)]
    est = 0
    for c in caches:
        spec = pl.BlockSpec((1,) + c.shape[1:], lambda i: (i, 0, 0))
        in_specs.append(spec)
        out_specs.append(spec)
        out_shapes.append(jax.ShapeDtypeStruct(c.shape, c.dtype))
        est += 6 * c.shape[1] * c.shape[2] * 4
    return pl.pallas_call(
        _sample_attn_kernel,
        grid=(nb,),
        in_specs=in_specs,
        out_specs=out_specs,
        out_shape=out_shapes,
        compiler_params=_params(1, est),
        name="sample_attn",
    )(qkv, *caches)


def _as_column(row_vec):
    n = row_vec.shape[1]
    eye = lax.broadcasted_iota(jnp.int32, (n, n), 0) == lax.broadcasted_iota(jnp.int32, (n, n), 1)
    return jnp.sum(jnp.where(eye, row_vec, 0.0), axis=1, keepdims=True)


def _sample_gla_kernel(q_ref, k_ref, v_ref, gk_ref, r_ref, gn_ref, st_ref, o_ref, new_ref):
    gn = gn_ref[...]
    for h in range(GLA_HEADS):
        dk = slice(h * GLA_HEAD_DK, (h + 1) * GLA_HEAD_DK)
        dv = slice(h * GLA_HEAD_DV, (h + 1) * GLA_HEAD_DV)
        decay = _as_column(jnp.exp(gk_ref[0, :, dk]))
        k_col = _as_column(k_ref[0, :, dk])
        q_col = _as_column(q_ref[0, :, dk])
        state = decay * st_ref[0, h] + k_col * v_ref[0, :, dv]
        new_ref[0, h] = state
        o = jnp.sum(q_col * state, axis=0, keepdims=True)
        rb = r_ref[0, :, dv]
        o_ref[0, :, dv] = _rms_normed(o, gn) * (rb * jax.nn.sigmoid(rb))


def _sample_gla(qb, kb, vb, gk, rb, gn, state):
    nb = qb.shape[0]

    def row_spec(width):
        return pl.BlockSpec((1, 1, width), lambda i: (i, 0, 0))

    state_spec = pl.BlockSpec((1,) + state.shape[1:], lambda i: (i, 0, 0, 0))
    return pl.pallas_call(
        _sample_gla_kernel,
        grid=(nb,),
        in_specs=[row_spec(GLA_DK), row_spec(GLA_DK), row_spec(GLA_DV), row_spec(GLA_DK), row_spec(GLA_DV),
                  pl.BlockSpec((1, GLA_HEAD_DV), lambda i: (0, 0)), state_spec],
        out_specs=[row_spec(GLA_DV), state_spec],
        out_shape=[jax.ShapeDtypeStruct((nb, 1, GLA_DV), _F32), jax.ShapeDtypeStruct(state.shape, state.dtype)],
        compiler_params=pltpu.CompilerParams(dimension_semantics=("arbitrary",)),
        name="sample_gla",
    )(qb, kb, vb, gk, rb, gn, state)


def _merge_ffn_kernel(x_ref, oa_ref, ob_ref, ga_ref, gb_ref, wpa_ref, wpb_ref, wo_ref, g2_ref, wup_ref, wdn_ref,
                      gf_ref, y_ref):
    tm = x_ref.shape[0]
    sub = min(tm, ROW_SUBTILE)
    for r0 in range(0, tm, sub):
        rows = slice(r0, r0 + sub)
        ya = jnp.dot(oa_ref[rows, :].astype(_BF16), wpa_ref[...], preferred_element_type=_F32)
        yb = jnp.dot(ob_ref[rows, :].astype(_BF16), wpb_ref[...], preferred_element_type=_F32)
        mix = (jax.nn.sigmoid(ga_ref[rows, :].astype(_F32)) * ya
               + jax.nn.sigmoid(gb_ref[rows, :].astype(_F32)) * yb)
        h = x_ref[rows, :] + jnp.dot(mix.astype(_BF16), wo_ref[...], preferred_element_type=_F32)
        hn = _rms_normed(h, g2_ref[...]).astype(_BF16)
        acc = h
        for c0 in range(0, D_FF, FF_CHUNK):
            f = jnp.maximum(jnp.dot(hn, wup_ref[:, c0:c0 + FF_CHUNK], preferred_element_type=_F32), 0.0)
            acc = acc + jnp.dot((f * f).astype(_BF16), wdn_ref[c0:c0 + FF_CHUNK, :], preferred_element_type=_F32)
        y_ref[rows, :] = _rms_normed(acc, gf_ref[...])


def _merge_ffn(x, oa, ob, ga, gb, wpa, wpb, wo, g2, wup, wdn, gf, tm):
    m, d = x.shape

    def rows(width):
        return pl.BlockSpec((tm, width), lambda i: (i, 0))

    weights = (wpa, wpb, wo, wup, wdn)
    est = (sum(w.size for w in weights) * 2 + 2 * tm * (2 * d * 4 + A_GROUP_WIDTH * 4 + 3 * d * 4)
           + ROW_SUBTILE * (6 * d + 2 * FF_CHUNK) * 4)
    return pl.pallas_call(
        _merge_ffn_kernel,
        grid=(m // tm,),
        in_specs=[rows(d), rows(A_GROUP_WIDTH), rows(GLA_DV), rows(d), rows(d),
                  _resident(wpa.shape), _resident(wpb.shape), _resident(wo.shape), _resident((1, d)),
                  _resident(wup.shape), _resident(wdn.shape), _resident((1, d))],
        out_specs=rows(d),
        out_shape=jax.ShapeDtypeStruct((m, d), _F32),
        compiler_params=_params(1, est),
        name="merge_ffn",
    )(x, oa, ob, ga, gb, wpa, wpb, wo, g2, wup, wdn, gf)


def _prepare_w_in(w_in):
    w = w_in.astype(_BF16)
    return w, w[:, _GATES_OFFSET:_GATES_OFFSET + 2 * D_MODEL]


def kernel(x_prompt, x_sample, cache_a1_k, cache_a1_v, cache_a2_k, cache_a2_v, cache_a3_k, cache_a3_v,
           state_gla, g_norm1, w_in, w_gk2, b_gk, g_gla, w_pa, w_pb, w_o, g_norm2, w_up, w_down, g_final):
    assert g_norm1.shape[0] == 1, "one layer"
    b, t, d = x_prompt.shape
    nb = x_sample.shape[0]
    assert x_sample.shape[1] == 1, "one new token per sample row"

    w_all, w_gates = _prepare_w_in(w_in[0])
    w_gk = jnp.pad(w_gk2[0], ((0, LANES - GLA_GATE_RANK), (0, 0))).astype(_BF16)
    b_gk_row = b_gk[0][None, :]
    g1 = g_norm1[0][None, :]
    g2 = g_norm2[0][None, :]
    gf = g_final[None, :]
    gn = g_gla[0][None, :]
    wpa = w_pa[0].astype(_BF16)
    wpb = w_pb[0].astype(_BF16)
    wo = w_o[0].astype(_BF16)
    wup = w_up[0].astype(_BF16)
    wdn = w_down[0].astype(_BF16)

    cos_p, sin_p = _rope_tables(t, 0, 1)
    outs = _attn_proj_prompt(x_prompt, g1, w_all, cos_p, sin_p)
    up, packs, p_caches = outs[0], outs[1:4], outs[4:]
    xp = x_prompt.reshape(b * t, d)
    ga, gb, ob, p_gla = _gla_fused(up.reshape(b * t, d), w_all, w_gates, w_gk, b_gk_row, gn, t // GLA_ROW_TILE)
    oa = _dilated_attn(packs, t)
    y_prompt = _merge_ffn(xp, oa.reshape(b * t, A_GROUP_WIDTH), ob, ga, gb,
                          wpa, wpb, wo, g2, wup, wdn, gf, ROW_TILE).reshape(b, t, d)

    cos_s, sin_s = _rope_tables(nb, PAST_LEN, 0)
    qkv_s = _attn_proj_sample(x_sample.reshape(1, nb, d), g1, w_all, cos_s, sin_s).reshape(nb, 1, 3 * A_WIDTH)
    xs = x_sample.reshape(nb, d)
    qb_s, kb_s, vb_s, rb_s, ga_s, gb_s, gk_s = _gla_proj(xs, g1, w_all, w_gates, w_gk, b_gk_row)
    caches = [jnp.transpose(c[0], (0, 2, 3, 1)).reshape(nb, A_GROUP_WIDTH, c.shape[2])
              for c in (cache_a1_k, cache_a1_v, cache_a2_k, cache_a2_v, cache_a3_k, cache_a3_v)]
    s_outs = _sample_attn(qkv_s, caches)
    oa_s, s_caches = s_outs[0], s_outs[1:]

    def tok(z):
        return z.reshape(nb, 1, z.shape[-1])

    ob_s, s_gla = _sample_gla(tok(qb_s), tok(kb_s), tok(vb_s), tok(gk_s), tok(rb_s), gn, state_gla[0])
    y_sample = _merge_ffn(xs, oa_s.reshape(nb, A_GROUP_WIDTH), ob_s.reshape(nb, GLA_DV), ga_s, gb_s,
                          wpa, wpb, wo, g2, wup, wdn, gf, nb).reshape(nb, 1, d)

    def prompt_cache_out(z):
        return z.reshape(1, z.shape[0], z.shape[1], A_HEADS_PER_GROUP, A_HEAD_DIM)

    def sample_cache_out(z):
        z = z.reshape(z.shape[0], A_HEADS_PER_GROUP, A_HEAD_DIM, z.shape[2])
        return jnp.transpose(z, (0, 3, 1, 2))[None]

    return (y_prompt, y_sample, *[prompt_cache_out(z) for z in p_caches], p_gla[None],
            *[sample_cache_out(z) for z in s_caches], s_gla[None])
```

```python
import functools

import jax
import jax.numpy as jnp
from jax import lax
from jax.experimental import pallas as pl
from jax.experimental.pallas import tpu as pltpu

D_MODEL = 1024
PAST_LEN = 16384
A_GROUPS = ((128, 1), (512, 4), (2048, 16))
A_N_GROUPS = 3
A_HEADS_PER_GROUP = 4
A_HEAD_DIM = 64
A_GROUP_WIDTH = A_HEADS_PER_GROUP * A_HEAD_DIM
A_WIDTH = A_N_GROUPS * A_GROUP_WIDTH
A_WINDOW_KEYS = 128
ROPE_THETA = 10000.0
GLA_HEADS = 4
GLA_HEAD_DK = 128
GLA_HEAD_DV = 256
GLA_DK = GLA_HEADS * GLA_HEAD_DK
GLA_DV = GLA_HEADS * GLA_HEAD_DV
GLA_GATE_RANK = 16
GLA_GATE_NORM = 16.0
D_FF = 4 * D_MODEL
EPS = 1e-6
LOG2_E = 1.4426950408889634

LANES = 128
V7X_SCOPED_VMEM_CAP = 56 * 1024 * 1024

ROW_TILE = 512
ROW_SUBTILE = 256
Q_BLOCK = 128
GLA_CHUNK = 128
GLA_ROW_TILE = 512
GLA_FACTOR_RANGE = 40.0
FF_CHUNK = 1024

_F32 = jnp.float32
_BF16 = jnp.bfloat16
_NEG_INF = float("-inf")


def _vmem_limit(nbytes):
    return int(min(V7X_SCOPED_VMEM_CAP, max(16 * 1024 * 1024, nbytes * 3 // 2)))


def _params(n_grid_dims, est_bytes):
    return pltpu.CompilerParams(
        dimension_semantics=("arbitrary",) * n_grid_dims, vmem_limit_bytes=_vmem_limit(est_bytes))


def _resident(shape):
    nd = len(shape)
    return pl.BlockSpec(shape, lambda *_: (0,) * nd, pipeline_mode=pl.Buffered(1))


def _rms_normed(x, g):
    return x * lax.rsqrt(jnp.mean(x * x, axis=-1, keepdims=True) + EPS) * g


def _rope_table_kernel(inv_ref, sign_ref, cos_ref, sin_ref, *, pos0, pos_step):
    rows = cos_ref.shape[0]
    row = lax.broadcasted_iota(jnp.int32, (rows, LANES), 0) + pl.program_id(0) * rows
    pos = (pos0 + row * pos_step).astype(_F32)
    ang = pos * inv_ref[...]
    cos_ref[...] = jnp.cos(ang)
    sin_ref[...] = jnp.sin(ang) * sign_ref[...]


def _rope_tables(n_rows, pos0, pos_step):
    half = A_HEAD_DIM // 2
    lane = jnp.arange(LANES)
    inv = ROPE_THETA ** (-((lane % half).astype(_F32)) / half)
    sign = jnp.where((lane % A_HEAD_DIM) < half, -1.0, 1.0).astype(_F32)
    tile = min(n_rows, ROW_TILE)
    return pl.pallas_call(
        functools.partial(_rope_table_kernel, pos0=pos0, pos_step=pos_step),
        grid=(n_rows // tile,),
        in_specs=[pl.BlockSpec((1, LANES), lambda i: (0, 0))] * 2,
        out_specs=[pl.BlockSpec((tile, LANES), lambda i: (i, 0))] * 2,
        out_shape=[jax.ShapeDtypeStruct((n_rows, LANES), _F32)] * 2,
        name="rope_table",
    )(inv[None, :], sign[None, :])


def _rope_slab(z, cos, sin_signed):
    half = A_HEAD_DIM // 2
    lane = lax.broadcasted_iota(jnp.int32, z.shape, 1)
    partner = jnp.where((lane % A_HEAD_DIM) < half, pltpu.roll(z, LANES - half, 1), pltpu.roll(z, half, 1))
    return z * cos + partner * sin_signed


_N_SLABS = 3 * A_WIDTH // LANES
_SLABS_PER_SECTION = A_WIDTH // LANES
_SLABS_PER_GROUP = A_GROUP_WIDTH // LANES


def _attn_proj_slabs(u, w_ref, cos, sin, q_scale):
    y = jnp.dot(u, w_ref[...], preferred_element_type=_F32)
    for s in range(_N_SLABS):
        z = y[:, s * LANES:(s + 1) * LANES]
        if s < _SLABS_PER_SECTION:
            z = _rope_slab(z, cos, sin) * q_scale
        elif s < 2 * _SLABS_PER_SECTION:
            z = _rope_slab(z, cos, sin)
        yield s, z


def _attn_proj_prompt_kernel(x_ref, g_ref, w_ref, cos_ref, sin_ref,
                             u_ref, p1_ref, p2_ref, p3_ref, c1k_ref, c1v_ref, c2k_ref, c2v_ref, c3k_ref, c3v_ref,
                             slab_ref, *, tm, n_tiles):
    i = pl.program_id(1)
    packs = (p1_ref, p2_ref, p3_ref)
    caches = ((c1k_ref, c1v_ref), (c2k_ref, c2v_ref), (c3k_ref, c3v_ref))
    sub = min(tm, ROW_SUBTILE)
    for r0 in range(0, tm, sub):
        rows = slice(r0, r0 + sub)
        u_ref[0, rows, :] = _rms_normed(x_ref[0, rows, :], g_ref[...]).astype(_BF16)
        for s, z in _attn_proj_slabs(u_ref[0, rows, :], w_ref, cos_ref[rows, :], sin_ref[rows, :], A_HEAD_DIM ** -0.5 * LOG2_E):
            slab_ref[s, rows, :] = z
            section, within = divmod(s, _SLABS_PER_SECTION)
            group, pair = divmod(within, _SLABS_PER_GROUP)
            dil = A_GROUPS[group][1]
            col = (section * _SLABS_PER_GROUP + pair) * LANES
            for r in range(dil):
                picked = slab_ref[s, pl.ds(r0 + r, sub // dil, stride=dil), :] if dil > 1 else z
                packs[group][0, r, r0 // dil:(r0 + sub) // dil, col:col + LANES] = picked.astype(_BF16)

    for group, (window, _) in enumerate(A_GROUPS):
        keep = min(window, tm)
        first_kept_tile = n_tiles - max(window // tm, 1)

        @pl.when(i >= first_kept_tile)
        def _(group=group, keep=keep):
            for section in (1, 2):
                for pair in range(_SLABS_PER_GROUP):
                    s = section * _SLABS_PER_SECTION + group * _SLABS_PER_GROUP + pair
                    caches[group][section - 1][0, :, pair * LANES:(pair + 1) * LANES] = slab_ref[s, tm - keep:tm, :]


def _attn_proj_sample_kernel(x_ref, g_ref, w_ref, cos_ref, sin_ref, o_ref):
    u = _rms_normed(x_ref[0], g_ref[...]).astype(_BF16)
    for s, z in _attn_proj_slabs(u, w_ref, cos_ref[...], sin_ref[...], A_HEAD_DIM ** -0.5):
        o_ref[0, :, s * LANES:(s + 1) * LANES] = z


def _attn_proj_prompt(x, g, w, cos, sin):
    b, t, d = x.shape
    tm = ROW_TILE
    n_tiles = t // tm
    pack_shapes, pack_specs = [], []
    for _, dil in A_GROUPS:
        pack_shapes.append(jax.ShapeDtypeStruct((b, dil, t // dil, 3 * A_GROUP_WIDTH), _BF16))
        pack_specs.append(pl.BlockSpec((1, dil, tm // dil, 3 * A_GROUP_WIDTH), lambda bi, i: (bi, 0, i, 0)))
    cache_shapes, cache_specs = [], []
    for window, _ in A_GROUPS:
        keep = min(window, t)
        blk = min(keep, tm)
        first = n_tiles - max(keep // tm, 1)
        for _ in range(2):
            cache_shapes.append(jax.ShapeDtypeStruct((b, keep, A_GROUP_WIDTH), _F32))
            cache_specs.append(pl.BlockSpec(
                (1, blk, A_GROUP_WIDTH), lambda bi, i, first=first: (bi, jnp.maximum(i - first, 0), 0)))
    est = (d * 3 * A_WIDTH * 2 + 2 * tm * d * 4 + 2 * 3 * tm * 3 * A_GROUP_WIDTH * 2
           + 2 * 6 * tm * A_GROUP_WIDTH * 4 + 2 * tm * 3 * A_WIDTH * 4)
    return pl.pallas_call(
        functools.partial(_attn_proj_prompt_kernel, tm=tm, n_tiles=n_tiles),
        grid=(b, n_tiles),
        in_specs=[
            pl.BlockSpec((1, tm, d), lambda bi, i: (bi, i, 0)),
            _resident((1, d)),
            _resident((d, 3 * A_WIDTH)),
            pl.BlockSpec((tm, LANES), lambda bi, i: (i, 0)),
            pl.BlockSpec((tm, LANES), lambda bi, i: (i, 0)),
        ],
        out_specs=[pl.BlockSpec((1, tm, d), lambda bi, i: (bi, i, 0))] + pack_specs + cache_specs,
        out_shape=[jax.ShapeDtypeStruct((b, t, d), _BF16)] + pack_shapes + cache_shapes,
        scratch_shapes=[pltpu.VMEM((_N_SLABS, tm, LANES), _F32)],
        compiler_params=_params(2, est),
        name="attn_proj_prompt",
    )(x, g, w, cos, sin)


def _attn_proj_sample(x, g, w, cos, sin):
    _, m, d = x.shape
    est = d * 3 * A_WIDTH * 2 * 2 + 4 * m * 3 * A_WIDTH * 4
    return pl.pallas_call(
        _attn_proj_sample_kernel,
        grid=(1,),
        in_specs=[
            pl.BlockSpec((1, m, d), lambda i: (0, 0, 0)),
            pl.BlockSpec((1, d), lambda i: (0, 0)),
            pl.BlockSpec((d, 3 * A_WIDTH), lambda i: (0, 0)),
            pl.BlockSpec((m, LANES), lambda i: (0, 0)),
            pl.BlockSpec((m, LANES), lambda i: (0, 0)),
        ],
        out_specs=pl.BlockSpec((1, m, 3 * A_WIDTH), lambda i: (0, 0, 0)),
        out_shape=jax.ShapeDtypeStruct((1, m, 3 * A_WIDTH), _F32),
        compiler_params=pltpu.CompilerParams(vmem_limit_bytes=_vmem_limit(est)),
        name="attn_proj_sample",
    )(x, g, w, cos, sin)


_GP_SECTIONS = {}
_off = 0
for _name, _width in (("qb", GLA_DK), ("kb", GLA_DK), ("vb", GLA_DV), ("rb", GLA_DV), ("glr", LANES)):
    _GP_SECTIONS[_name] = (_off, _width)
    _off += _width
_GP_OFFSET, _GP_WIDTH = 3 * A_WIDTH, _off
_GATES_OFFSET = 3 * A_WIDTH + 2 * GLA_DK + 2 * GLA_DV + GLA_GATE_RANK


def _gla_project(u, w_ref, wg_ref, wgk_ref, bgk_ref):
    def section(name):
        off, width = _GP_SECTIONS[name]
        return jnp.dot(u, w_ref[:, off:off + width], preferred_element_type=_F32)

    glr = section("glr").astype(_BF16)
    z = jnp.dot(glr, wgk_ref[...], preferred_element_type=_F32) + bgk_ref[...]
    log_sigmoid = jnp.minimum(z, 0.0) - jnp.log(1.0 + jnp.exp(-jnp.abs(z)))
    yield "gk", log_sigmoid / GLA_GATE_NORM
    yield "qb", section("qb") * (GLA_HEAD_DK ** -0.5)
    for name in ("kb", "vb", "rb"):
        yield name, section(name)
    yield "ga", jnp.dot(u, wg_ref[:, :D_MODEL], preferred_element_type=_F32)
    yield "gb", jnp.dot(u, wg_ref[:, D_MODEL:], preferred_element_type=_F32)


def _gla_proj_kernel(x_ref, g_ref, w_ref, wg_ref, wgk_ref, bgk_ref,
                     qb_ref, kb_ref, vb_ref, rb_ref, ga_ref, gb_ref, gk_ref):
    outs = dict(qb=qb_ref, kb=kb_ref, vb=vb_ref, rb=rb_ref, ga=ga_ref, gb=gb_ref, gk=gk_ref)
    u = _rms_normed(x_ref[...], g_ref[...]).astype(_BF16)
    for name, y in _gla_project(u, w_ref, wg_ref, wgk_ref, bgk_ref):
        outs[name][...] = y.astype(outs[name].dtype)


def _gla_proj(x, g, w, w_gates, wgk, bgk):
    m, d = x.shape
    widths = (GLA_DK, GLA_DK, GLA_DV, GLA_DV, D_MODEL, D_MODEL, GLA_DK)
    est = (w.size + w_gates.size) * 2 + 4 * m * (d + sum(widths)) * 4

    def whole(shape):
        return pl.BlockSpec(shape, lambda i: (0,) * len(shape))

    return pl.pallas_call(
        _gla_proj_kernel,
        grid=(1,),
        in_specs=[whole(x.shape), whole(g.shape), _resident(w.shape), _resident(w_gates.shape), whole(wgk.shape),
                  whole(bgk.shape)],
        out_specs=[whole((m, wd)) for wd in widths],
        out_shape=[jax.ShapeDtypeStruct((m, wd), _F32) for wd in widths],
        compiler_params=_params(1, est),
        name="gla_proj",
    )(x, g, w, w_gates, wgk, bgk)


def _window_scores(q, k, bias, head0):
    zero = jnp.zeros_like(q)
    q_heads = jnp.concatenate([jnp.where(head0, q, zero), jnp.where(head0, zero, q)], axis=0)
    s = lax.dot_general(q_heads, k, (((1,), (1,)), ((), ())), preferred_element_type=_F32)
    return s + jnp.concatenate([bias, bias], axis=0)


def _window_values(s, v, head0):
    qb = s.shape[0] // 2
    m = jnp.max(s, axis=1, keepdims=True)
    p = jnp.exp2(s - m)
    l = jnp.sum(p, axis=1, keepdims=True)
    num = jnp.dot(p.astype(_BF16), v, preferred_element_type=_F32)
    return (jnp.where(head0, m[:qb], m[qb:]), jnp.where(head0, l[:qb], l[qb:]),
            jnp.where(head0, num[:qb], num[qb:]))


def _pairwise_decay_scores(q, k, cum):
    c = q.shape[0]
    row = lax.broadcasted_iota(jnp.int32, (c, c), 0)
    col = lax.broadcasted_iota(jnp.int32, (c, c), 1)
    row_d = lax.broadcasted_iota(jnp.int32, q.shape, 0)

    def sub_diagonal(delta, acc):
        k_back = pltpu.roll(k, delta, 0)
        cum_back = pltpu.roll(cum, delta, 0)
        log_decay = jnp.where(row_d >= delta, cum - cum_back, 0.0)
        diag = jnp.sum(q * k_back * jnp.exp2(log_decay), axis=1, keepdims=True)
        return acc + jnp.where(row - col == delta, diag, 0.0)

    return lax.fori_loop(0, c, sub_diagonal, jnp.zeros((c, c), _F32))


def _gla_tile(q_ref, k_ref, v_ref, gk_ref, r_ref, gn, o_ref, st_ref, cum_ref, sc_ref, between_units=None):
    c = GLA_CHUNK
    n_chunks = q_ref.shape[0] // c
    row = lax.broadcasted_iota(jnp.int32, (c, c), 0)
    col = lax.broadcasted_iota(jnp.int32, (c, c), 1)
    causal = row >= col
    causal_bf = jnp.where(causal, 1.0, 0.0).astype(_BF16)

    def head_cols(h):
        return (slice(h * GLA_HEAD_DK, (h + 1) * GLA_HEAD_DK), slice(h * GLA_HEAD_DV, (h + 1) * GLA_HEAD_DV))

    spread = jnp.zeros((1, GLA_DK), _F32)
    for ci in range(n_chunks):
        rows = slice(ci * c, (ci + 1) * c)
        g = gk_ref[rows, :]
        g_hi = g.astype(_BF16)
        g_lo = (g - g_hi.astype(_F32)).astype(_BF16)
        parts = jnp.dot(causal_bf, jnp.concatenate([g_hi, g_lo], axis=1), preferred_element_type=_F32)
        cum = (parts[:, :GLA_DK] + parts[:, GLA_DK:]) * LOG2_E
        cum_ref[rows, :] = cum
        mid = cum[c // 2 - 1:c // 2, :]
        spread = jnp.maximum(spread, jnp.maximum(cum[0:1, :] - mid, mid - cum[c - 1:c, :]))
    factorable = jnp.max(spread) <= GLA_FACTOR_RANGE * LOG2_E

    @pl.when(factorable)
    def _():
        for ci in range(n_chunks):
            rows = slice(ci * c, (ci + 1) * c)
            for h in range(GLA_HEADS):
                dk, _ = head_cols(h)
                cum = cum_ref[rows, dk]
                mid = cum[c // 2 - 1:c // 2, :]
                q_mid = (q_ref[rows, dk].astype(_F32) * jnp.exp2(cum - mid)).astype(_BF16)
                k_mid = (k_ref[rows, dk].astype(_F32) * jnp.exp2(mid - cum)).astype(_BF16)
                scores = lax.dot_general(q_mid, k_mid, (((1,), (1,)), ((), ())), preferred_element_type=_F32)
                sc_ref[ci * GLA_HEADS + h] = jnp.where(causal, scores, 0.0).astype(_BF16)

    @pl.when(jnp.logical_not(factorable))
    def _():
        for ci in range(n_chunks):
            rows = slice(ci * c, (ci + 1) * c)
            for h in range(GLA_HEADS):
                dk, _ = head_cols(h)
                scores = _pairwise_decay_scores(q_ref[rows, dk].astype(_F32), k_ref[rows, dk].astype(_F32),
                                                cum_ref[rows, dk])
                sc_ref[ci * GLA_HEADS + h] = scores.astype(_BF16)

    states = [st_ref[h] for h in range(GLA_HEADS)]
    for ci in range(n_chunks):
        rows = slice(ci * c, (ci + 1) * c)
        for h in range(GLA_HEADS):
            dk, dv = head_cols(h)
            cum = cum_ref[rows, dk]
            last = cum[c - 1:c, :]
            v = v_ref[rows, dv]
            q_in = (q_ref[rows, dk].astype(_F32) * jnp.exp2(cum)).astype(_BF16)
            k_out = (k_ref[rows, dk].astype(_F32) * jnp.exp2(last - cum)).astype(_BF16)
            o = jnp.dot(sc_ref[ci * GLA_HEADS + h], v, preferred_element_type=_F32)
            o = o + lax.dot_general(q_in, states[h].astype(_BF16), (((1,), (1,)), ((), ())),
                                    preferred_element_type=_F32)
            update_t = lax.dot_general(v, k_out, (((0,), (0,)), ((), ())), preferred_element_type=_F32)
            states[h] = states[h] * jnp.exp2(last) + update_t
            rb = r_ref[rows, dv].astype(_F32)
            o = _rms_normed(o, gn) * (rb * jax.nn.sigmoid(rb))
            o_ref[rows, dv] = o.astype(o_ref.dtype)
            if between_units is not None:
                between_units(ci * GLA_HEADS + h, n_chunks * GLA_HEADS)
    for h in range(GLA_HEADS):
        st_ref[h] = states[h]


_G3_DIL = A_GROUPS[2][1]
_G3_HALO_BLOCKS = A_WINDOW_KEYS // (GLA_ROW_TILE // _G3_DIL)


def _init_window_biases(bias12_ref, bias3_ref):
    qb = Q_BLOCK
    row = lax.broadcasted_iota(jnp.int32, (qb, 2 * qb), 0)
    col = lax.broadcasted_iota(jnp.int32, (qb, 2 * qb), 1)
    back = row - col + qb
    in_window = (back >= 0) & (back <= A_WINDOW_KEYS)
    bias12_ref[0] = jnp.where(in_window, 0.0, _NEG_INF)
    bias12_ref[1] = jnp.where(in_window & (col >= qb), 0.0, _NEG_INF)
    q3 = GLA_ROW_TILE // _G3_DIL
    n_k = (_G3_HALO_BLOCKS + 1) * q3
    row = lax.broadcasted_iota(jnp.int32, (q3, n_k), 0)
    col = lax.broadcasted_iota(jnp.int32, (q3, n_k), 1)
    back = row - col + _G3_HALO_BLOCKS * q3
    in_window = (back >= 0) & (back <= A_WINDOW_KEYS)
    for n in range(_G3_HALO_BLOCKS + 1):
        bias3_ref[n] = jnp.where(in_window & (col >= (_G3_HALO_BLOCKS - n) * q3), 0.0, _NEG_INF)


ATTN_PIPELINE_DEPTH = 4


def _tile_attention_units(tile_in_seq, q_refs, k_refs, v_refs, bias12_ref, bias3_ref, mln_refs, o_ref):
    qb = Q_BLOCK
    q3 = q_refs[2].shape[2]
    dil2 = A_GROUPS[1][1]
    n_pairs = A_GROUP_WIDTH // LANES
    first = (tile_in_seq == 0).astype(jnp.int32)
    bias3_index = jnp.minimum(tile_in_seq, _G3_HALO_BLOCKS)

    def head0_of(n_q):
        return lax.broadcasted_iota(jnp.int32, (n_q, LANES), 1) < A_HEAD_DIM

    def rows_of(ref, r, lanes):
        return ref[0, r, :, lanes]

    blocks = []
    for pair in range(n_pairs):
        lanes = slice(pair * LANES, (pair + 1) * LANES)
        for r in range(_G3_DIL):
            blocks.append((pair, 2, r, q3,
                           functools.partial(rows_of, q_refs[2], r, lanes),
                           lambda r=r, lanes=lanes: jnp.concatenate([rows_of(ref, r, lanes) for ref in k_refs[2]], 0),
                           lambda r=r, lanes=lanes: jnp.concatenate([rows_of(ref, r, lanes) for ref in v_refs[2]], 0),
                           lambda: bias3_ref[bias3_index]))
        for r in range(dil2):
            blocks.append((pair, 1, r, qb,
                           functools.partial(rows_of, q_refs[1], r, lanes),
                           lambda r=r, lanes=lanes: jnp.concatenate([rows_of(ref, r, lanes) for ref in k_refs[1]], 0),
                           lambda r=r, lanes=lanes: jnp.concatenate([rows_of(ref, r, lanes) for ref in v_refs[1]], 0),
                           lambda: bias12_ref[first]))
        for n in range(q_refs[0].shape[2] // qb):
            def window(refs, n=n, lanes=lanes):
                halo, cur = refs
                if n == 0:
                    return jnp.concatenate([halo[0, 0, :, lanes], cur[0, 0, :qb, lanes]], axis=0)
                return cur[0, 0, (n - 1) * qb:(n + 1) * qb, lanes]
            blocks.append((pair, 0, n, qb,
                           lambda n=n, lanes=lanes: q_refs[0][0, 0, n * qb:(n + 1) * qb, lanes],
                           functools.partial(window, k_refs[0]), functools.partial(window, v_refs[0]),
                           (lambda: bias12_ref[first]) if n == 0 else (lambda: bias12_ref[0])))

    scores, stats = {}, {}
    for i in range(len(blocks) + ATTN_PIPELINE_DEPTH):
        if i < len(blocks):
            pair, group, idx, n_q, load_q, load_k, _, load_bias = blocks[i]
            scores[i] = _window_scores(load_q(), load_k(), load_bias(), head0_of(n_q))
            yield
        done = i - ATTN_PIPELINE_DEPTH
        if done >= 0:
            pair, group, idx, n_q, _, _, load_v, _ = blocks[done]
            stats[(pair, group, idx)] = _window_values(scores.pop(done), load_v(), head0_of(n_q))
            yield

    def merged(old, new):
        (m_old, l_old, n_old), (m_g, l_g, n_g) = old, new
        m_new = jnp.maximum(m_old, m_g)
        a = jnp.exp2(m_old - m_new)
        b = jnp.exp2(m_g - m_new)
        return m_new, a * l_old + b * l_g, a * n_old + b * n_g

    for pair in range(n_pairs):
        lanes = slice(pair * LANES, (pair + 1) * LANES)
        m_ref, l_ref, n_ref = mln_refs[3 * pair:3 * pair + 3]
        for r in range(_G3_DIL):
            rows = pl.ds(r, q3, stride=_G3_DIL)
            m_ref[rows, :], l_ref[rows, :], n_ref[rows, :] = stats[(pair, 2, r)]
        rows2 = [pl.ds(r, qb, stride=dil2) for r in range(dil2)]
        old2 = [(m_ref[rows, :], l_ref[rows, :], n_ref[rows, :]) for rows in rows2]
        for r, (rows, old) in enumerate(zip(rows2, old2)):
            m_ref[rows, :], l_ref[rows, :], n_ref[rows, :] = merged(old, stats[(pair, 1, r)])
        rows1 = [slice(n * qb, (n + 1) * qb) for n in range(q_refs[0].shape[2] // qb)]
        old1 = [(m_ref[rows, :], l_ref[rows, :], n_ref[rows, :]) for rows in rows1]
        for n, (rows, old) in enumerate(zip(rows1, old1)):
            _, l_new, n_new = merged(old, stats[(pair, 0, n)])
            o_ref[rows, lanes] = (n_new / l_new).astype(o_ref.dtype)
        yield


def _gla_fused_kernel(u_ref, w_ref, wg_ref, wgk_ref, bgk_ref, gn_ref, *rest, tiles_per_seq):
    n_attn = 3 + 2 * (2 + 2 + _G3_HALO_BLOCKS + 1)
    attn_refs, rest = rest[:n_attn], rest[n_attn:]
    ga_ref, gb_ref, o_ref, oa_ref, fin_ref = rest[:5]
    qs_ref, ks_ref, vs_ref, rs_ref, gks_ref, st_ref, cum_ref, sc_ref, bias12_ref, bias3_ref = rest[5:15]
    mln_refs = rest[15:]
    q1, k1h, k1c, v1h, v1c, q2, k2p, k2c, v2p, v2c, q3 = attn_refs[:11]
    k3 = attn_refs[11:12 + _G3_HALO_BLOCKS]
    v3 = attn_refs[12 + _G3_HALO_BLOCKS:]

    j = pl.program_id(0)
    tm = u_ref.shape[0]
    write_half = j % 2
    read_half = 1 - write_half
    gla_tile = jnp.maximum(j - 1, 0)
    new_tile = jnp.minimum(j, pl.num_programs(0) - 2)
    projected = dict(qb=qs_ref, kb=ks_ref, vb=vs_ref, rb=rs_ref, gk=gks_ref)

    @pl.when(j == 0)
    def _():
        for ref in projected.values():
            ref[...] = jnp.zeros_like(ref)
        _init_window_biases(bias12_ref, bias3_ref)

    @pl.when(gla_tile % tiles_per_seq == 0)
    def _():
        st_ref[...] = jnp.zeros_like(st_ref)

    def projection_units():
        sub = min(tm, ROW_SUBTILE)
        for r0 in range(0, tm, sub):
            rows = slice(r0, r0 + sub)
            for name, y in _gla_project(u_ref[rows, :], w_ref, wg_ref, wgk_ref, bgk_ref):
                if name == "ga":
                    ga_ref[rows, :] = y.astype(ga_ref.dtype)
                elif name == "gb":
                    gb_ref[rows, :] = y.astype(gb_ref.dtype)
                else:
                    projected[name][write_half, rows, :] = y.astype(projected[name].dtype)
                yield

    n_pairs = A_GROUP_WIDTH // LANES
    streams = [
        projection_units(),
        _tile_attention_units(new_tile % tiles_per_seq, (q1, q2, q3), ((k1h, k1c), (k2p, k2c), k3),
                              ((v1h, v1c), (v2p, v2c), v3), bias12_ref, bias3_ref, mln_refs, oa_ref),
    ]
    quotas = [7 * (tm // min(tm, ROW_SUBTILE)),
              2 * n_pairs * (_G3_DIL + A_GROUPS[1][1] + tm // Q_BLOCK) + n_pairs]
    issued = [0] * len(streams)

    def between_units(i, n):
        for s, (gen, quota) in enumerate(zip(streams, quotas)):
            while issued[s] < quota * (i + 1) // n:
                next(gen, None)
                issued[s] += 1

    _gla_tile(qs_ref.at[read_half], ks_ref.at[read_half], vs_ref.at[read_half], gks_ref.at[read_half],
              rs_ref.at[read_half], gn_ref[...], o_ref, st_ref, cum_ref, sc_ref, between_units=between_units)
    for gen in streams:
        for _ in gen:
            pass

    @pl.when((gla_tile % tiles_per_seq == tiles_per_seq - 1) & (j >= 1))
    def _():
        for h in range(GLA_HEADS):
            fin_ref[0, h] = st_ref[h].T


def _gla_fused(u, packs, w, w_gates, wgk, bgk, gn, tiles_per_seq):
    m, d = u.shape
    tm = GLA_ROW_TILE
    n_tiles = m // tm
    n_seq = n_tiles // tiles_per_seq
    assert tm == ROW_TILE and tm % (_G3_DIL * 16) == 0 and A_WINDOW_KEYS % (tm // _G3_DIL) == 0

    def new_tile(j):
        return jnp.minimum(j, n_tiles - 1)

    def proj_tile(width):
        return pl.BlockSpec((tm, width), lambda j: (new_tile(j), 0))

    def gla_tile(width):
        return pl.BlockSpec((tm, width), lambda j: (jnp.maximum(j - 1, 0), 0))

    def pack_block(group, section, rows, blocks_back):
        dil = A_GROUPS[group][1]
        per_tile = tm // dil // rows

        def index(j):
            t = new_tile(j)
            own_last = (t % tiles_per_seq + 1) * per_tile - 1
            return (t // tiles_per_seq, 0, jnp.maximum(own_last - blocks_back, 0), section)

        return pl.BlockSpec((1, dil, rows, A_GROUP_WIDTH), index)

    qb = Q_BLOCK
    q3 = tm // _G3_DIL
    attn_specs, attn_operands = [], []

    def add(group, section, rows, blocks_back):
        attn_specs.append(pack_block(group, section, rows, blocks_back))
        attn_operands.append(packs[group])

    add(0, 0, tm, 0)
    for section in (1, 2):
        add(0, section, qb, tm // qb)
        add(0, section, tm, 0)
    add(1, 0, qb, 0)
    for section in (1, 2):
        add(1, section, qb, 1)
        add(1, section, qb, 0)
    add(2, 0, q3, 0)
    for section in (1, 2):
        for back in range(_G3_HALO_BLOCKS, -1, -1):
            add(2, section, q3, back)

    est = ((w.size + w_gates.size) * 2 + 2 * tm * (d * 2 + 2 * D_MODEL * 2 + GLA_DV * 2)
           + 2 * tm * (2 * GLA_DK * 2 + 2 * GLA_DV * 2 + GLA_DK * 4) + tm * GLA_DK * 4
           + 4 * GLA_HEADS * GLA_HEAD_DK * GLA_HEAD_DV * 4 + 2 * ROW_SUBTILE * GLA_DV * 4
           + 2 * (3 * tm + 2 * (tm + 3 * qb) + 2 * (_G3_HALO_BLOCKS + 1) * tm) * A_GROUP_WIDTH * 2
           + 6 * tm * LANES * 4 + 2 * tm * A_GROUP_WIDTH * 2)
    return pl.pallas_call(
        functools.partial(_gla_fused_kernel, tiles_per_seq=tiles_per_seq),
        grid=(n_tiles + 1,),
        in_specs=[proj_tile(d), _resident(w.shape), _resident(w_gates.shape), _resident(wgk.shape),
                  _resident(bgk.shape), _resident(gn.shape)] + attn_specs,
        out_specs=[
            proj_tile(D_MODEL), proj_tile(D_MODEL), gla_tile(GLA_DV), proj_tile(A_GROUP_WIDTH),
            pl.BlockSpec((1, GLA_HEADS, GLA_HEAD_DK, GLA_HEAD_DV),
                         lambda j: (jnp.maximum(j - 1, 0) // tiles_per_seq, 0, 0, 0)),
        ],
        out_shape=[
            jax.ShapeDtypeStruct((m, D_MODEL), _BF16), jax.ShapeDtypeStruct((m, D_MODEL), _BF16),
            jax.ShapeDtypeStruct((m, GLA_DV), _BF16), jax.ShapeDtypeStruct((m, A_GROUP_WIDTH), _BF16),
            jax.ShapeDtypeStruct((n_seq, GLA_HEADS, GLA_HEAD_DK, GLA_HEAD_DV), _F32),
        ],
        scratch_shapes=[
            pltpu.VMEM((2, tm, GLA_DK), _BF16), pltpu.VMEM((2, tm, GLA_DK), _BF16),
            pltpu.VMEM((2, tm, GLA_DV), _BF16), pltpu.VMEM((2, tm, GLA_DV), _BF16),
            pltpu.VMEM((2, tm, GLA_DK), _F32),
            pltpu.VMEM((GLA_HEADS, GLA_HEAD_DV, GLA_HEAD_DK), _F32),
            pltpu.VMEM((tm, GLA_DK), _F32),
            pltpu.VMEM((tm // GLA_CHUNK * GLA_HEADS, GLA_CHUNK, GLA_CHUNK), _BF16),
            pltpu.VMEM((2, qb, 2 * qb), _F32),
            pltpu.VMEM((_G3_HALO_BLOCKS + 1, q3, (_G3_HALO_BLOCKS + 1) * q3), _F32),
        ] + [pltpu.VMEM((tm, LANES), _F32)] * (3 * (A_GROUP_WIDTH // LANES)),
        compiler_params=_params(1, est),
        name="gla_fused",
    )(u, w, w_gates, wgk, bgk, gn, *attn_operands)


def _sample_attn_kernel(qkv_ref, c1k_ref, c1v_ref, c2k_ref, c2v_ref, c3k_ref, c3v_ref,
                        o_ref, n1k_ref, n1v_ref, n2k_ref, n2v_ref, n3k_ref, n3v_ref):
    caches = ((c1k_ref, c1v_ref, n1k_ref, n1v_ref), (c2k_ref, c2v_ref, n2k_ref, n2v_ref),
              (c3k_ref, c3v_ref, n3k_ref, n3v_ref))
    head_rows = 8
    hrow = lax.broadcasted_iota(jnp.int32, (head_rows, A_GROUP_WIDTH), 0)
    hlane = lax.broadcasted_iota(jnp.int32, (head_rows, A_GROUP_WIDTH), 1)
    own = (hlane // A_HEAD_DIM) == hrow
    stats = []
    for group, (kc_ref, vc_ref, kn_ref, vn_ref) in enumerate(caches):
        window, dil = A_GROUPS[group]
        base = group * A_GROUP_WIDTH
        q = qkv_ref[0, :, base:base + A_GROUP_WIDTH]
        k_new = qkv_ref[0, :, A_WIDTH + base:A_WIDTH + base + A_GROUP_WIDTH]
        v_new = qkv_ref[0, :, 2 * A_WIDTH + base:2 * A_WIDTH + base + A_GROUP_WIDTH]
        kc = kc_ref[0]
        vc = vc_ref[0]
        is_last = lax.broadcasted_iota(jnp.int32, (A_GROUP_WIDTH, window), 1) == window - 1
        kn_ref[0] = jnp.where(is_last, _as_column(k_new), pltpu.roll(kc, window - 1, 1))
        vn_ref[0] = jnp.where(is_last, _as_column(v_new), pltpu.roll(vc, window - 1, 1))

        q_heads = jnp.where(own, q, 0.0)
        s_old = jnp.dot(q_heads.astype(_BF16), kc.astype(_BF16), preferred_element_type=_F32)
        pos = lax.broadcasted_iota(jnp.int32, (head_rows, window), 1)
        s_old = jnp.where(pos % dil == 0, s_old, _NEG_INF)
        s_new = jnp.sum(q_heads * k_new, axis=1, keepdims=True)
        m = jnp.maximum(jnp.max(s_old, axis=1, keepdims=True), s_new)
        p_old = jnp.exp(s_old - m)
        p_new = jnp.exp(s_new - m)
        l = jnp.sum(p_old, axis=1, keepdims=True) + p_new
        num = lax.dot_general(p_old.astype(_BF16), vc.astype(_BF16), (((1,), (1,)), ((), ())),
                              preferred_element_type=_F32) + p_new * v_new
        stats.append((m, l, num))
    m_all = jnp.maximum(jnp.maximum(stats[0][0], stats[1][0]), stats[2][0])
    num = sum(jnp.exp(m - m_all) * n for m, _, n in stats)
    den = sum(jnp.exp(m - m_all) * l for m, l, _ in stats)
    o = jnp.where(own, num / den, 0.0)
    o_ref[0] = jnp.sum(o, axis=0, keepdims=True)


def _sample_attn(qkv, caches):
    nb = qkv.shape[0]
    in_specs = [pl.BlockSpec((1, 1, 3 * A_WIDTH), lambda i: (i, 0, 0))]
    out_specs = [pl.BlockSpec((1, 1, A_GROUP_WIDTH), lambda i: (i, 0, 0))]
    out_shapes = [jax.ShapeDtypeStruct((nb, 1, A_GROUP_WIDTH), _F32)]
    est = 0
    for c in caches:
        spec = pl.BlockSpec((1,) + c.shape[1:], lambda i: (i, 0, 0))
        in_specs.append(spec)
        out_specs.append(spec)
        out_shapes.append(jax.ShapeDtypeStruct(c.shape, c.dtype))
        est += 6 * c.shape[1] * c.shape[2] * 4
    return pl.pallas_call(
        _sample_attn_kernel,
        grid=(nb,),
        in_specs=in_specs,
        out_specs=out_specs,
        out_shape=out_shapes,
        compiler_params=_params(1, est),
        name="sample_attn",
    )(qkv, *caches)


def _as_column(row_vec):
    n = row_vec.shape[1]
    eye = lax.broadcasted_iota(jnp.int32, (n, n), 0) == lax.broadcasted_iota(jnp.int32, (n, n), 1)
    return jnp.sum(jnp.where(eye, row_vec, 0.0), axis=1, keepdims=True)


def _sample_gla_kernel(q_ref, k_ref, v_ref, gk_ref, r_ref, gn_ref, st_ref, o_ref, new_ref):
    gn = gn_ref[...]
    for h in range(GLA_HEADS):
        dk = slice(h * GLA_HEAD_DK, (h + 1) * GLA_HEAD_DK)
        dv = slice(h * GLA_HEAD_DV, (h + 1) * GLA_HEAD_DV)
        decay = _as_column(jnp.exp(gk_ref[0, :, dk]))
        k_col = _as_column(k_ref[0, :, dk])
        q_col = _as_column(q_ref[0, :, dk])
        state = decay * st_ref[0, h] + k_col * v_ref[0, :, dv]
        new_ref[0, h] = state
        o = jnp.sum(q_col * state, axis=0, keepdims=True)
        rb = r_ref[0, :, dv]
        o_ref[0, :, dv] = _rms_normed(o, gn) * (rb * jax.nn.sigmoid(rb))


def _sample_gla(qb, kb, vb, gk, rb, gn, state):
    nb = qb.shape[0]

    def row_spec(width):
        return pl.BlockSpec((1, 1, width), lambda i: (i, 0, 0))

    state_spec = pl.BlockSpec((1,) + state.shape[1:], lambda i: (i, 0, 0, 0))
    return pl.pallas_call(
        _sample_gla_kernel,
        grid=(nb,),
        in_specs=[row_spec(GLA_DK), row_spec(GLA_DK), row_spec(GLA_DV), row_spec(GLA_DK), row_spec(GLA_DV),
                  pl.BlockSpec((1, GLA_HEAD_DV), lambda i: (0, 0)), state_spec],
        out_specs=[row_spec(GLA_DV), state_spec],
        out_shape=[jax.ShapeDtypeStruct((nb, 1, GLA_DV), _F32), jax.ShapeDtypeStruct(state.shape, state.dtype)],
        compiler_params=pltpu.CompilerParams(dimension_semantics=("arbitrary",)),
        name="sample_gla",
    )(qb, kb, vb, gk, rb, gn, state)


def _merge_ffn_kernel(x_ref, oa_ref, ob_ref, ga_ref, gb_ref, wpa_ref, wpb_ref, wo_ref, g2_ref, wup_ref, wdn_ref,
                      gf_ref, y_ref):
    tm = x_ref.shape[0]
    sub = min(tm, ROW_SUBTILE)
    for r0 in range(0, tm, sub):
        rows = slice(r0, r0 + sub)
        ya = jnp.dot(oa_ref[rows, :].astype(_BF16), wpa_ref[...], preferred_element_type=_F32)
        yb = jnp.dot(ob_ref[rows, :].astype(_BF16), wpb_ref[...], preferred_element_type=_F32)
        mix = (jax.nn.sigmoid(ga_ref[rows, :].astype(_F32)) * ya
               + jax.nn.sigmoid(gb_ref[rows, :].astype(_F32)) * yb)
        h = x_ref[rows, :] + jnp.dot(mix.astype(_BF16), wo_ref[...], preferred_element_type=_F32)
        hn = _rms_normed(h, g2_ref[...]).astype(_BF16)
        acc = h
        for c0 in range(0, D_FF, FF_CHUNK):
            f = jnp.maximum(jnp.dot(hn, wup_ref[:, c0:c0 + FF_CHUNK], preferred_element_type=_F32), 0.0)
            acc = acc + jnp.dot((f * f).astype(_BF16), wdn_ref[c0:c0 + FF_CHUNK, :], preferred_element_type=_F32)
        y_ref[rows, :] = _rms_normed(acc, gf_ref[...])


def _merge_ffn(x, oa, ob, ga, gb, wpa, wpb, wo, g2, wup, wdn, gf, tm):
    m, d = x.shape

    def rows(width):
        return pl.BlockSpec((tm, width), lambda i: (i, 0))

    weights = (wpa, wpb, wo, wup, wdn)
    est = (sum(w.size for w in weights) * 2 + 2 * tm * (2 * d * 4 + A_GROUP_WIDTH * 4 + 3 * d * 4)
           + ROW_SUBTILE * (6 * d + 2 * FF_CHUNK) * 4)
    return pl.pallas_call(
        _merge_ffn_kernel,
        grid=(m // tm,),
        in_specs=[rows(d), rows(A_GROUP_WIDTH), rows(GLA_DV), rows(d), rows(d),
                  _resident(wpa.shape), _resident(wpb.shape), _resident(wo.shape), _resident((1, d)),
                  _resident(wup.shape), _resident(wdn.shape), _resident((1, d))],
        out_specs=rows(d),
        out_shape=jax.ShapeDtypeStruct((m, d), _F32),
        compiler_params=_params(1, est),
        name="merge_ffn",
    )(x, oa, ob, ga, gb, wpa, wpb, wo, g2, wup, wdn, gf)


def _prepare_w_in(w_in):
    w = w_in.astype(_BF16)
    return w, w[:, _GP_OFFSET:_GP_OFFSET + _GP_WIDTH], w[:, _GATES_OFFSET:_GATES_OFFSET + 2 * D_MODEL]


def kernel(x_prompt, x_sample, cache_a1_k, cache_a1_v, cache_a2_k, cache_a2_v, cache_a3_k, cache_a3_v,
           state_gla, g_norm1, w_in, w_gk2, b_gk, g_gla, w_pa, w_pb, w_o, g_norm2, w_up, w_down, g_final):
    assert g_norm1.shape[0] == 1, "one layer"
    b, t, d = x_prompt.shape
    nb = x_sample.shape[0]
    assert x_sample.shape[1] == 1, "one new token per sample row"

    w_all, w_gla, w_gates = _prepare_w_in(w_in[0])
    w_gk = jnp.pad(w_gk2[0], ((0, LANES - GLA_GATE_RANK), (0, 0))).astype(_BF16)
    b_gk_row = b_gk[0][None, :]
    g1 = g_norm1[0][None, :]
    g2 = g_norm2[0][None, :]
    gf = g_final[None, :]
    gn = g_gla[0][None, :]
    wpa = w_pa[0].astype(_BF16)
    wpb = w_pb[0].astype(_BF16)
    wo = w_o[0].astype(_BF16)
    wup = w_up[0].astype(_BF16)
    wdn = w_down[0].astype(_BF16)

    cos_p, sin_p = _rope_tables(t, 0, 1)
    outs = _attn_proj_prompt(x_prompt, g1, w_all, cos_p, sin_p)
    up, packs, p_caches = outs[0], outs[1:4], outs[4:]
    xp = x_prompt.reshape(b * t, d)
    ga, gb, ob, oa, p_gla = _gla_fused(up.reshape(b * t, d), packs, w_gla, w_gates, w_gk, b_gk_row, gn,
                                       t // GLA_ROW_TILE)
    y_prompt = _merge_ffn(xp, oa, ob, ga, gb,
                          wpa, wpb, wo, g2, wup, wdn, gf, ROW_TILE).reshape(b, t, d)

    cos_s, sin_s = _rope_tables(nb, PAST_LEN, 0)
    qkv_s = _attn_proj_sample(x_sample.reshape(1, nb, d), g1, w_all, cos_s, sin_s).reshape(nb, 1, 3 * A_WIDTH)
    xs = x_sample.reshape(nb, d)
    qb_s, kb_s, vb_s, rb_s, ga_s, gb_s, gk_s = _gla_proj(xs, g1, w_gla, w_gates, w_gk, b_gk_row)
    caches = [jnp.transpose(c[0], (0, 2, 3, 1)).reshape(nb, A_GROUP_WIDTH, c.shape[2])
              for c in (cache_a1_k, cache_a1_v, cache_a2_k, cache_a2_v, cache_a3_k, cache_a3_v)]
    s_outs = _sample_attn(qkv_s, caches)
    oa_s, s_caches = s_outs[0], s_outs[1:]

    def tok(z):
        return z.reshape(nb, 1, z.shape[-1])

    ob_s, s_gla = _sample_gla(tok(qb_s), tok(kb_s), tok(vb_s), tok(gk_s), tok(rb_s), gn, state_gla[0])
    y_sample = _merge_ffn(xs, oa_s.reshape(nb, A_GROUP_WIDTH), ob_s.reshape(nb, GLA_DV), ga_s, gb_s,
                          wpa, wpb, wo, g2, wup, wdn, gf, nb).reshape(nb, 1, d)

    def prompt_cache_out(z):
        return z.reshape(1, z.shape[0], z.shape[1], A_HEADS_PER_GROUP, A_HEAD_DIM)

    def sample_cache_out(z):
        z = z.reshape(z.shape[0], A_HEADS_PER_GROUP, A_HEAD_DIM, z.shape[2])
        return jnp.transpose(z, (0, 3, 1, 2))[None]

    return (y_prompt, y_sample, *[prompt_cache_out(z) for z in p_caches], p_gla[None],
            *[sample_cache_out(z) for z in s_caches], s_gla[None])
```

```python
import functools

import jax
import jax.numpy as jnp
from jax import lax
from jax.experimental import pallas as pl
from jax.experimental.pallas import tpu as pltpu

D_MODEL = 1024
PAST_LEN = 16384
A_GROUPS = ((128, 1), (512, 4), (2048, 16))
A_N_GROUPS = 3
A_HEADS_PER_GROUP = 4
A_HEAD_DIM = 64
A_GROUP_WIDTH = A_HEADS_PER_GROUP * A_HEAD_DIM
A_WIDTH = A_N_GROUPS * A_GROUP_WIDTH
A_WINDOW_KEYS = 128
ROPE_THETA = 10000.0
GLA_HEADS = 4
GLA_HEAD_DK = 128
GLA_HEAD_DV = 256
GLA_DK = GLA_HEADS * GLA_HEAD_DK
GLA_DV = GLA_HEADS * GLA_HEAD_DV
GLA_GATE_RANK = 16
GLA_GATE_NORM = 16.0
D_FF = 4 * D_MODEL
EPS = 1e-6
LOG2_E = 1.4426950408889634

LANES = 128
V7X_SCOPED_VMEM_CAP = 56 * 1024 * 1024

ROW_TILE = 512
ROW_SUBTILE = 256
MERGE_SUBTILE = 512
Q_BLOCK = 128
ATTN_BLOCKS_PER_STEP = 16
GLA_CHUNK = 128
GLA_ROW_TILE = 512
GLA_FACTOR_RANGE = 40.0
FF_CHUNK = 1024
SAMPLE_GLA_ROWS = 4

_F32 = jnp.float32
_BF16 = jnp.bfloat16
_NEG_INF = float("-inf")


def _vmem_limit(nbytes):
    return int(min(V7X_SCOPED_VMEM_CAP, max(16 * 1024 * 1024, nbytes * 3 // 2)))


def _params(n_grid_dims, est_bytes):
    return pltpu.CompilerParams(
        dimension_semantics=("arbitrary",) * n_grid_dims, vmem_limit_bytes=_vmem_limit(est_bytes))


def _resident(shape):
    nd = len(shape)
    return pl.BlockSpec(shape, lambda *_: (0,) * nd, pipeline_mode=pl.Buffered(1))


def _rms_normed(x, g):
    return x * lax.rsqrt(jnp.mean(x * x, axis=-1, keepdims=True) + EPS) * g


def _rope_table_kernel(inv_ref, sign_ref, cos_ref, sin_ref, *, pos0, pos_step):
    rows = cos_ref.shape[0]
    row = lax.broadcasted_iota(jnp.int32, (rows, LANES), 0) + pl.program_id(0) * rows
    pos = (pos0 + row * pos_step).astype(_F32)
    ang = pos * inv_ref[...]
    cos_ref[...] = jnp.cos(ang)
    sin_ref[...] = jnp.sin(ang) * sign_ref[...]


def _rope_tables(n_rows, pos0, pos_step):
    half = A_HEAD_DIM // 2
    lane = jnp.arange(LANES)
    inv = ROPE_THETA ** (-((lane % half).astype(_F32)) / half)
    sign = jnp.where((lane % A_HEAD_DIM) < half, -1.0, 1.0).astype(_F32)
    tile = min(n_rows, ROW_TILE)
    return pl.pallas_call(
        functools.partial(_rope_table_kernel, pos0=pos0, pos_step=pos_step),
        grid=(n_rows // tile,),
        in_specs=[pl.BlockSpec((1, LANES), lambda i: (0, 0))] * 2,
        out_specs=[pl.BlockSpec((tile, LANES), lambda i: (i, 0))] * 2,
        out_shape=[jax.ShapeDtypeStruct((n_rows, LANES), _F32)] * 2,
        name="rope_table",
    )(inv[None, :], sign[None, :])


def _rope_slab(z, cos, sin_signed):
    half = A_HEAD_DIM // 2
    lane = lax.broadcasted_iota(jnp.int32, z.shape, 1)
    partner = jnp.where((lane % A_HEAD_DIM) < half, pltpu.roll(z, LANES - half, 1), pltpu.roll(z, half, 1))
    return z * cos + partner * sin_signed


_N_SLABS = 3 * A_WIDTH // LANES
_SLABS_PER_SECTION = A_WIDTH // LANES
_SLABS_PER_GROUP = A_GROUP_WIDTH // LANES


def _attn_proj_slabs(u, w_ref, cos, sin, q_scale):
    y = jnp.dot(u, w_ref[...], preferred_element_type=_F32)
    for s in range(_N_SLABS):
        z = y[:, s * LANES:(s + 1) * LANES]
        if s < _SLABS_PER_SECTION:
            z = _rope_slab(z, cos, sin) * q_scale
        elif s < 2 * _SLABS_PER_SECTION:
            z = _rope_slab(z, cos, sin)
        yield s, z


def _attn_proj_prompt_kernel(x_ref, g_ref, w_ref, cos_ref, sin_ref,
                             u_ref, p1_ref, p2_ref, p3_ref, c1k_ref, c1v_ref, c2k_ref, c2v_ref, c3k_ref, c3v_ref,
                             slab_ref, *, tm, n_tiles):
    i = pl.program_id(1)
    packs = (p1_ref, p2_ref, p3_ref)
    caches = ((c1k_ref, c1v_ref), (c2k_ref, c2v_ref), (c3k_ref, c3v_ref))
    sub = min(tm, ROW_SUBTILE)
    for r0 in range(0, tm, sub):
        rows = slice(r0, r0 + sub)
        u_ref[0, rows, :] = _rms_normed(x_ref[0, rows, :], g_ref[...]).astype(_BF16)
        for s, z in _attn_proj_slabs(u_ref[0, rows, :], w_ref, cos_ref[rows, :], sin_ref[rows, :],
                                     A_HEAD_DIM ** -0.5 * LOG2_E):
            slab_ref[s, rows, :] = z
            section, within = divmod(s, _SLABS_PER_SECTION)
            group, pair = divmod(within, _SLABS_PER_GROUP)
            dil = A_GROUPS[group][1]
            col = (section * _SLABS_PER_GROUP + pair) * LANES
            for r in range(dil):
                picked = slab_ref[s, pl.ds(r0 + r, sub // dil, stride=dil), :] if dil > 1 else z
                packs[group][0, r, r0 // dil:(r0 + sub) // dil, col:col + LANES] = picked.astype(_BF16)

    for group, (window, _) in enumerate(A_GROUPS):
        keep = min(window, tm)
        first_kept_tile = n_tiles - max(window // tm, 1)

        @pl.when(i >= first_kept_tile)
        def _(group=group, keep=keep):
            for section in (1, 2):
                for pair in range(_SLABS_PER_GROUP):
                    s = section * _SLABS_PER_SECTION + group * _SLABS_PER_GROUP + pair
                    caches[group][section - 1][0, :, pair * LANES:(pair + 1) * LANES] = slab_ref[s, tm - keep:tm, :]


def _attn_proj_sample_kernel(x_ref, g_ref, w_ref, cos_ref, sin_ref, o_ref):
    u = _rms_normed(x_ref[0], g_ref[...]).astype(_BF16)
    for s, z in _attn_proj_slabs(u, w_ref, cos_ref[...], sin_ref[...], A_HEAD_DIM ** -0.5):
        o_ref[0, :, s * LANES:(s + 1) * LANES] = z


def _attn_proj_prompt(x, g, w, cos, sin):
    b, t, d = x.shape
    tm = ROW_TILE
    n_tiles = t // tm
    pack_shapes, pack_specs = [], []
    for _, dil in A_GROUPS:
        pack_shapes.append(jax.ShapeDtypeStruct((b, dil, t // dil, 3 * A_GROUP_WIDTH), _BF16))
        pack_specs.append(pl.BlockSpec((1, dil, tm // dil, 3 * A_GROUP_WIDTH), lambda bi, i: (bi, 0, i, 0)))
    cache_shapes, cache_specs = [], []
    for window, _ in A_GROUPS:
        keep = min(window, t)
        blk = min(keep, tm)
        first = n_tiles - max(keep // tm, 1)
        for _ in range(2):
            cache_shapes.append(jax.ShapeDtypeStruct((b, keep, A_GROUP_WIDTH), _F32))
            cache_specs.append(pl.BlockSpec(
                (1, blk, A_GROUP_WIDTH), lambda bi, i, first=first: (bi, jnp.maximum(i - first, 0), 0)))
    est = (d * 3 * A_WIDTH * 2 + 2 * tm * d * 4 + 2 * 3 * tm * 3 * A_GROUP_WIDTH * 2
           + 2 * 6 * tm * A_GROUP_WIDTH * 4 + 2 * tm * 3 * A_WIDTH * 4)
    return pl.pallas_call(
        functools.partial(_attn_proj_prompt_kernel, tm=tm, n_tiles=n_tiles),
        grid=(b, n_tiles),
        in_specs=[
            pl.BlockSpec((1, tm, d), lambda bi, i: (bi, i, 0)),
            _resident((1, d)),
            _resident((d, 3 * A_WIDTH)),
            pl.BlockSpec((tm, LANES), lambda bi, i: (i, 0)),
            pl.BlockSpec((tm, LANES), lambda bi, i: (i, 0)),
        ],
        out_specs=[pl.BlockSpec((1, tm, d), lambda bi, i: (bi, i, 0))] + pack_specs + cache_specs,
        out_shape=[jax.ShapeDtypeStruct((b, t, d), _BF16)] + pack_shapes + cache_shapes,
        scratch_shapes=[pltpu.VMEM((_N_SLABS, tm, LANES), _F32)],
        compiler_params=_params(2, est),
        name="attn_proj_prompt",
    )(x, g, w, cos, sin)


def _attn_proj_sample(x, g, w, cos, sin):
    _, m, d = x.shape
    est = d * 3 * A_WIDTH * 2 * 2 + 4 * m * 3 * A_WIDTH * 4
    return pl.pallas_call(
        _attn_proj_sample_kernel,
        grid=(1,),
        in_specs=[
            pl.BlockSpec((1, m, d), lambda i: (0, 0, 0)),
            pl.BlockSpec((1, d), lambda i: (0, 0)),
            pl.BlockSpec((d, 3 * A_WIDTH), lambda i: (0, 0)),
            pl.BlockSpec((m, LANES), lambda i: (0, 0)),
            pl.BlockSpec((m, LANES), lambda i: (0, 0)),
        ],
        out_specs=pl.BlockSpec((1, m, 3 * A_WIDTH), lambda i: (0, 0, 0)),
        out_shape=jax.ShapeDtypeStruct((1, m, 3 * A_WIDTH), _F32),
        compiler_params=pltpu.CompilerParams(vmem_limit_bytes=_vmem_limit(est)),
        name="attn_proj_sample",
    )(x, g, w, cos, sin)


_GP_SECTIONS = {}
_off = 3 * A_WIDTH
for _name, _width in (("qb", GLA_DK), ("kb", GLA_DK), ("vb", GLA_DV), ("rb", GLA_DV), ("glr", LANES)):
    _GP_SECTIONS[_name] = (_off, _width)
    _off += _width
_GATES_OFFSET = 3 * A_WIDTH + 2 * GLA_DK + 2 * GLA_DV + GLA_GATE_RANK


def _gla_project(u, w_ref, wg_ref, wgk_ref, bgk_ref):
    def section(name):
        off, width = _GP_SECTIONS[name]
        return jnp.dot(u, w_ref[:, off:off + width], preferred_element_type=_F32)

    glr = section("glr").astype(_BF16)
    z = jnp.dot(glr, wgk_ref[...], preferred_element_type=_F32) + bgk_ref[...]
    log_sigmoid = jnp.minimum(z, 0.0) - jnp.log(1.0 + jnp.exp(-jnp.abs(z)))
    yield "gk", log_sigmoid / GLA_GATE_NORM
    yield "qb", section("qb") * (GLA_HEAD_DK ** -0.5)
    for name in ("kb", "vb", "rb"):
        yield name, section(name)
    yield "ga", jnp.dot(u, wg_ref[:, :D_MODEL], preferred_element_type=_F32)
    yield "gb", jnp.dot(u, wg_ref[:, D_MODEL:], preferred_element_type=_F32)


def _gla_proj_kernel(x_ref, g_ref, w_ref, wg_ref, wgk_ref, bgk_ref,
                     qb_ref, kb_ref, vb_ref, rb_ref, ga_ref, gb_ref, gk_ref):
    outs = dict(qb=qb_ref, kb=kb_ref, vb=vb_ref, rb=rb_ref, ga=ga_ref, gb=gb_ref, gk=gk_ref)
    u = _rms_normed(x_ref[...], g_ref[...]).astype(_BF16)
    for name, y in _gla_project(u, w_ref, wg_ref, wgk_ref, bgk_ref):
        outs[name][...] = y.astype(outs[name].dtype)


def _gla_proj(x, g, w, w_gates, wgk, bgk):
    m, d = x.shape
    widths = (GLA_DK, GLA_DK, GLA_DV, GLA_DV, D_MODEL, D_MODEL, GLA_DK)
    est = (w.size + w_gates.size) * 2 + 4 * m * (d + sum(widths)) * 4

    def whole(shape):
        return pl.BlockSpec(shape, lambda i: (0,) * len(shape))

    return pl.pallas_call(
        _gla_proj_kernel,
        grid=(1,),
        in_specs=[whole(x.shape), whole(g.shape), _resident(w.shape), _resident(w_gates.shape), whole(wgk.shape),
                  whole(bgk.shape)],
        out_specs=[whole((m, wd)) for wd in widths],
        out_shape=[jax.ShapeDtypeStruct((m, wd), _F32) for wd in widths],
        compiler_params=_params(1, est),
        name="gla_proj",
    )(x, g, w, w_gates, wgk, bgk)


def _window_attention(q, k, v, bias, head0):
    qb = Q_BLOCK
    zero = jnp.zeros_like(q)
    q_heads = jnp.concatenate([jnp.where(head0, q, zero), jnp.where(head0, zero, q)], axis=0)
    s = lax.dot_general(q_heads, k, (((1,), (1,)), ((), ())), preferred_element_type=_F32)
    s = s + jnp.concatenate([bias, bias], axis=0)
    m = jnp.max(s, axis=1, keepdims=True)
    p = jnp.exp2(s - m)
    l = jnp.sum(p, axis=1, keepdims=True)
    num = jnp.dot(p.astype(_BF16), v, preferred_element_type=_F32)
    return (jnp.where(head0, m[:qb], m[qb:]), jnp.where(head0, l[:qb], l[qb:]),
            jnp.where(head0, num[:qb], num[qb:]))


def _dilated_attn_kernel(q1_ref, k1_ref, v1_ref, q2_ref, k2_ref, v2_ref, q3_ref, k3_ref, v3_ref,
                         o_ref, m_ref, l_ref, n_ref, bias_ref):
    qb = Q_BLOCK
    lane = lax.broadcasted_iota(jnp.int32, (qb, LANES), 1)
    head0 = lane < A_HEAD_DIM
    row = lax.broadcasted_iota(jnp.int32, (qb, 2 * qb), 0)
    col = lax.broadcasted_iota(jnp.int32, (qb, 2 * qb), 1)
    for j, offset in enumerate((0, qb)):
        diff = row - col + offset
        bias_ref[j] = jnp.where((diff >= 0) & (diff <= A_WINDOW_KEYS), 0.0, _NEG_INF)
    groups = ((q1_ref, k1_ref, v1_ref), (q2_ref, k2_ref, v2_ref), (q3_ref, k3_ref, v3_ref))
    first_group, last_group = A_N_GROUPS - 1, 0
    per_step = ATTN_BLOCKS_PER_STEP

    for group in range(A_N_GROUPS - 1, -1, -1):
        q_ref, k_ref, v_ref = groups[group]
        dil = A_GROUPS[group][1]
        n_blocks = q_ref.shape[2] // qb
        res_per_step = min(dil, per_step)
        blk_per_step = per_step // res_per_step
        blk_steps = n_blocks // blk_per_step

        def step(idx, carry, q_ref=q_ref, k_ref=k_ref, v_ref=v_ref, dil=dil, group=group,
                 res_per_step=res_per_step, blk_per_step=blk_per_step, blk_steps=blk_steps):
            stats, rows = [], []
            for u in range(per_step):
                if dil == 1:
                    r = 0
                else:
                    r = (idx // blk_steps) * res_per_step + u % res_per_step
                n = (idx % blk_steps) * blk_per_step + u // res_per_step
                q0 = pl.multiple_of(n * qb, qb)
                k0 = pl.multiple_of(jnp.maximum(n - 1, 0) * qb, qb)
                q = q_ref[0, r, pl.ds(q0, qb), :]
                k = k_ref[0, r, pl.ds(k0, 2 * qb), :]
                v = v_ref[0, r, pl.ds(k0, 2 * qb), :]
                stats.append(_window_attention(q, k, v, bias_ref[jnp.minimum(n, 1)], head0))
                rows.append(pl.ds(q0, qb) if dil == 1 else pl.ds(q0 * dil + r, qb, stride=dil))
            if group == first_group:
                for (m_g, l_g, n_g), rw in zip(stats, rows):
                    m_ref[rw, :] = m_g
                    l_ref[rw, :] = l_g
                    n_ref[rw, :] = n_g
                return carry
            old = [(m_ref[rw, :], l_ref[rw, :], n_ref[rw, :]) for rw in rows]
            for (m_g, l_g, n_g), (m_old, l_old, n_old), rw in zip(stats, old, rows):
                m_new = jnp.maximum(m_old, m_g)
                a = jnp.exp2(m_old - m_new)
                b = jnp.exp2(m_g - m_new)
                l_new = a * l_old + b * l_g
                n_new = a * n_old + b * n_g
                if group == last_group:
                    o_ref[0, rw, :] = (n_new / l_new).astype(o_ref.dtype)
                else:
                    m_ref[rw, :] = m_new
                    l_ref[rw, :] = l_new
                    n_ref[rw, :] = n_new
            return carry

        lax.fori_loop(0, dil * n_blocks // per_step, step, 0)


def _dilated_attn(packs, t):
    b = packs[0].shape[0]
    n_pairs = A_GROUP_WIDTH // LANES
    in_specs, operands = [], []
    for pack, (_, dil) in zip(packs, A_GROUPS):
        for section in range(3):
            in_specs.append(pl.BlockSpec(
                (1, dil, t // dil, LANES), lambda bi, p, section=section: (bi, 0, 0, section * n_pairs + p)))
            operands.append(pack)
    est = 2 * 9 * t * LANES * 2 + 2 * t * LANES * 2 + 3 * t * LANES * 4
    return pl.pallas_call(
        _dilated_attn_kernel,
        grid=(b, n_pairs),
        in_specs=in_specs,
        out_specs=pl.BlockSpec((1, t, LANES), lambda bi, p: (bi, 0, p)),
        out_shape=jax.ShapeDtypeStruct((b, t, A_GROUP_WIDTH), _BF16),
        scratch_shapes=[pltpu.VMEM((t, LANES), _F32)] * 3 + [pltpu.VMEM((2, Q_BLOCK, 2 * Q_BLOCK), _F32)],
        compiler_params=_params(2, est),
        name="dilated_attn",
    )(*operands)


def _pairwise_decay_scores(q, k, cum):
    c = q.shape[0]
    row = lax.broadcasted_iota(jnp.int32, (c, c), 0)
    col = lax.broadcasted_iota(jnp.int32, (c, c), 1)
    row_d = lax.broadcasted_iota(jnp.int32, q.shape, 0)

    def sub_diagonal(delta, acc):
        k_back = pltpu.roll(k, delta, 0)
        cum_back = pltpu.roll(cum, delta, 0)
        log_decay = jnp.where(row_d >= delta, cum - cum_back, 0.0)
        diag = jnp.sum(q * k_back * jnp.exp2(log_decay), axis=1, keepdims=True)
        return acc + jnp.where(row - col == delta, diag, 0.0)

    return lax.fori_loop(0, c, sub_diagonal, jnp.zeros((c, c), _F32))


def _gla_tile(q_ref, k_ref, v_ref, gk_ref, r_ref, gn, o_ref, st_ref, cum_ref, sc_ref, interleaved=()):
    c = GLA_CHUNK
    n_chunks = q_ref.shape[0] // c
    row = lax.broadcasted_iota(jnp.int32, (c, c), 0)
    col = lax.broadcasted_iota(jnp.int32, (c, c), 1)
    causal = row >= col
    causal_bf = jnp.where(causal, 1.0, 0.0).astype(_BF16)

    def head_cols(h):
        return (slice(h * GLA_HEAD_DK, (h + 1) * GLA_HEAD_DK), slice(h * GLA_HEAD_DV, (h + 1) * GLA_HEAD_DV))

    spread = jnp.zeros((1, GLA_DK), _F32)
    for ci in range(n_chunks):
        rows = slice(ci * c, (ci + 1) * c)
        g = gk_ref[rows, :]
        g_hi = g.astype(_BF16)
        g_lo = (g - g_hi.astype(_F32)).astype(_BF16)
        parts = jnp.dot(causal_bf, jnp.concatenate([g_hi, g_lo], axis=1), preferred_element_type=_F32)
        cum = (parts[:, :GLA_DK] + parts[:, GLA_DK:]) * LOG2_E
        cum_ref[rows, :] = cum
        mid = cum[c // 2 - 1:c // 2, :]
        spread = jnp.maximum(spread, jnp.maximum(cum[0:1, :] - mid, mid - cum[c - 1:c, :]))
    factorable = jnp.max(spread) <= GLA_FACTOR_RANGE * LOG2_E

    @pl.when(factorable)
    def _():
        for ci in range(n_chunks):
            rows = slice(ci * c, (ci + 1) * c)
            for h in range(GLA_HEADS):
                dk, _ = head_cols(h)
                cum = cum_ref[rows, dk]
                mid = cum[c // 2 - 1:c // 2, :]
                q_mid = (q_ref[rows, dk].astype(_F32) * jnp.exp2(cum - mid)).astype(_BF16)
                k_mid = (k_ref[rows, dk].astype(_F32) * jnp.exp2(mid - cum)).astype(_BF16)
                scores = lax.dot_general(q_mid, k_mid, (((1,), (1,)), ((), ())), preferred_element_type=_F32)
                sc_ref[ci * GLA_HEADS + h] = jnp.where(causal, scores, 0.0).astype(_BF16)

    @pl.when(jnp.logical_not(factorable))
    def _():
        for ci in range(n_chunks):
            rows = slice(ci * c, (ci + 1) * c)
            for h in range(GLA_HEADS):
                dk, _ = head_cols(h)
                scores = _pairwise_decay_scores(q_ref[rows, dk].astype(_F32), k_ref[rows, dk].astype(_F32),
                                                cum_ref[rows, dk])
                sc_ref[ci * GLA_HEADS + h] = scores.astype(_BF16)

    interleaved = list(interleaved)
    states = [st_ref[h] for h in range(GLA_HEADS)]
    for ci in range(n_chunks):
        rows = slice(ci * c, (ci + 1) * c)
        for h in range(GLA_HEADS):
            dk, dv = head_cols(h)
            cum = cum_ref[rows, dk]
            last = cum[c - 1:c, :]
            v = v_ref[rows, dv]
            q_in = (q_ref[rows, dk].astype(_F32) * jnp.exp2(cum)).astype(_BF16)
            k_out = (k_ref[rows, dk].astype(_F32) * jnp.exp2(last - cum)).astype(_BF16)
            o = jnp.dot(sc_ref[ci * GLA_HEADS + h], v, preferred_element_type=_F32)
            o = o + lax.dot_general(q_in, states[h].astype(_BF16), (((1,), (1,)), ((), ())),
                                    preferred_element_type=_F32)
            update_t = lax.dot_general(v, k_out, (((0,), (0,)), ((), ())), preferred_element_type=_F32)
            states[h] = states[h] * jnp.exp2(last) + update_t
            rb = r_ref[rows, dv].astype(_F32)
            o = _rms_normed(o, gn) * (rb * jax.nn.sigmoid(rb))
            o_ref[rows, dv] = o.astype(o_ref.dtype)
        if ci < len(interleaved):
            interleaved[ci]()
    for work in interleaved[n_chunks:]:
        work()
    for h in range(GLA_HEADS):
        st_ref[h] = states[h]


def _gla_fused_kernel(u_ref, w_ref, wg_ref, wgk_ref, bgk_ref, gn_ref,
                      ga_ref, gb_ref, o_ref, fin_ref,
                      qs_ref, ks_ref, vs_ref, rs_ref, gks_ref, st_ref, cum_ref, sc_ref, *, tiles_per_seq):
    j = pl.program_id(0)
    tm = u_ref.shape[0]
    write_half = j % 2
    read_half = 1 - write_half
    gla_tile = jnp.maximum(j - 1, 0)
    projected = dict(qb=qs_ref, kb=ks_ref, vb=vs_ref, rb=rs_ref, gk=gks_ref)

    @pl.when(j == 0)
    def _():
        for ref in projected.values():
            ref[...] = jnp.zeros_like(ref)

    @pl.when(gla_tile % tiles_per_seq == 0)
    def _():
        st_ref[...] = jnp.zeros_like(st_ref)

    def project(r0, sub):
        def run():
            rows = slice(r0, r0 + sub)
            for name, y in _gla_project(u_ref[rows, :], w_ref, wg_ref, wgk_ref, bgk_ref):
                if name == "ga":
                    ga_ref[rows, :] = y.astype(ga_ref.dtype)
                elif name == "gb":
                    gb_ref[rows, :] = y.astype(gb_ref.dtype)
                else:
                    projected[name][write_half, rows, :] = y.astype(projected[name].dtype)
        return run

    sub = min(tm, ROW_SUBTILE)
    _gla_tile(qs_ref.at[read_half], ks_ref.at[read_half], vs_ref.at[read_half], gks_ref.at[read_half],
              rs_ref.at[read_half], gn_ref[...], o_ref, st_ref, cum_ref, sc_ref,
              interleaved=[project(r0, sub) for r0 in range(0, tm, sub)])

    @pl.when((gla_tile % tiles_per_seq == tiles_per_seq - 1) & (j >= 1))
    def _():
        for h in range(GLA_HEADS):
            fin_ref[0, h] = st_ref[h].T


def _gla_fused(u, w, w_gates, wgk, bgk, gn, tiles_per_seq):
    m, d = u.shape
    tm = GLA_ROW_TILE
    n_tiles = m // tm
    n_seq = n_tiles // tiles_per_seq

    def proj_tile(width):
        return pl.BlockSpec((tm, width), lambda j: (jnp.minimum(j, n_tiles - 1), 0))

    def gla_tile(width):
        return pl.BlockSpec((tm, width), lambda j: (jnp.maximum(j - 1, 0), 0))

    est = ((w.size + w_gates.size) * 2 + 2 * tm * (d * 2 + 2 * D_MODEL * 2 + GLA_DV * 2)
           + 2 * tm * (2 * GLA_DK * 2 + 2 * GLA_DV * 2 + GLA_DK * 4) + tm * GLA_DK * 4
           + 4 * GLA_HEADS * GLA_HEAD_DK * GLA_HEAD_DV * 4 + 2 * ROW_SUBTILE * GLA_DV * 4)
    return pl.pallas_call(
        functools.partial(_gla_fused_kernel, tiles_per_seq=tiles_per_seq),
        grid=(n_tiles + 1,),
        in_specs=[proj_tile(d), _resident(w.shape), _resident(w_gates.shape), _resident(wgk.shape),
                  _resident(bgk.shape), _resident(gn.shape)],
        out_specs=[
            proj_tile(D_MODEL), proj_tile(D_MODEL), gla_tile(GLA_DV),
            pl.BlockSpec((1, GLA_HEADS, GLA_HEAD_DK, GLA_HEAD_DV),
                         lambda j: (jnp.maximum(j - 1, 0) // tiles_per_seq, 0, 0, 0)),
        ],
        out_shape=[
            jax.ShapeDtypeStruct((m, D_MODEL), _BF16), jax.ShapeDtypeStruct((m, D_MODEL), _BF16),
            jax.ShapeDtypeStruct((m, GLA_DV), _BF16),
            jax.ShapeDtypeStruct((n_seq, GLA_HEADS, GLA_HEAD_DK, GLA_HEAD_DV), _F32),
        ],
        scratch_shapes=[
            pltpu.VMEM((2, tm, GLA_DK), _BF16), pltpu.VMEM((2, tm, GLA_DK), _BF16),
            pltpu.VMEM((2, tm, GLA_DV), _BF16), pltpu.VMEM((2, tm, GLA_DV), _BF16),
            pltpu.VMEM((2, tm, GLA_DK), _F32),
            pltpu.VMEM((GLA_HEADS, GLA_HEAD_DV, GLA_HEAD_DK), _F32),
            pltpu.VMEM((tm, GLA_DK), _F32),
            pltpu.VMEM((tm // GLA_CHUNK * GLA_HEADS, GLA_CHUNK, GLA_CHUNK), _BF16),
        ],
        compiler_params=_params(1, est),
        name="gla_fused",
    )(u, w, w_gates, wgk, bgk, gn)


def _sample_attn_kernel(qkv_ref, c1k_ref, c1v_ref, c2k_ref, c2v_ref, c3k_ref, c3v_ref,
                        o_ref, n1k_ref, n1v_ref, n2k_ref, n2v_ref, n3k_ref, n3v_ref):
    caches = ((c1k_ref, c1v_ref, n1k_ref, n1v_ref), (c2k_ref, c2v_ref, n2k_ref, n2v_ref),
              (c3k_ref, c3v_ref, n3k_ref, n3v_ref))
    head_rows = 8
    hrow = lax.broadcasted_iota(jnp.int32, (head_rows, A_GROUP_WIDTH), 0)
    hlane = lax.broadcasted_iota(jnp.int32, (head_rows, A_GROUP_WIDTH), 1)
    own = (hlane // A_HEAD_DIM) == hrow
    stats = []
    for group, (kc_ref, vc_ref, kn_ref, vn_ref) in enumerate(caches):
        window, dil = A_GROUPS[group]
        base = group * A_GROUP_WIDTH
        q = qkv_ref[0, :, base:base + A_GROUP_WIDTH]
        k_new = qkv_ref[0, :, A_WIDTH + base:A_WIDTH + base + A_GROUP_WIDTH]
        v_new = qkv_ref[0, :, 2 * A_WIDTH + base:2 * A_WIDTH + base + A_GROUP_WIDTH]
        kc = kc_ref[0]
        vc = vc_ref[0]
        is_last = lax.broadcasted_iota(jnp.int32, (A_GROUP_WIDTH, window), 1) == window - 1
        kn_ref[0] = jnp.where(is_last, _as_column(k_new), pltpu.roll(kc, window - 1, 1))
        vn_ref[0] = jnp.where(is_last, _as_column(v_new), pltpu.roll(vc, window - 1, 1))

        q_heads = jnp.where(own, q, 0.0)
        s_old = jnp.dot(q_heads.astype(_BF16), kc.astype(_BF16), preferred_element_type=_F32)
        pos = lax.broadcasted_iota(jnp.int32, (head_rows, window), 1)
        s_old = jnp.where(pos % dil == 0, s_old, _NEG_INF)
        s_new = jnp.sum(q_heads * k_new, axis=1, keepdims=True)
        m = jnp.maximum(jnp.max(s_old, axis=1, keepdims=True), s_new)
        p_old = jnp.exp(s_old - m)
        p_new = jnp.exp(s_new - m)
        l = jnp.sum(p_old, axis=1, keepdims=True) + p_new
        num = lax.dot_general(p_old.astype(_BF16), vc.astype(_BF16), (((1,), (1,)), ((), ())),
                              preferred_element_type=_F32) + p_new * v_new
        stats.append((m, l, num))
    m_all = jnp.maximum(jnp.maximum(stats[0][0], stats[1][0]), stats[2][0])
    num = sum(jnp.exp(m - m_all) * n for m, _, n in stats)
    den = sum(jnp.exp(m - m_all) * l for m, l, _ in stats)
    o = jnp.where(own, num / den, 0.0)
    o_ref[0] = jnp.sum(o, axis=0, keepdims=True)


def _sample_attn(qkv, caches):
    nb = qkv.shape[0]
    in_specs = [pl.BlockSpec((1, 1, 3 * A_WIDTH), lambda i: (i, 0, 0))]
    out_specs = [pl.BlockSpec((1, 1, A_GROUP_WIDTH), lambda i: (i, 0, 0))]
    out_shapes = [jax.ShapeDtypeStruct((nb, 1, A_GROUP_WIDTH), _F32)]
    est = 0
    for c in caches:
        spec = pl.BlockSpec((1,) + c.shape[1:], lambda i: (i, 0, 0))
        in_specs.append(spec)
        out_specs.append(spec)
        out_shapes.append(jax.ShapeDtypeStruct(c.shape, c.dtype))
        est += 6 * c.shape[1] * c.shape[2] * 4
    return pl.pallas_call(
        _sample_attn_kernel,
        grid=(nb,),
        in_specs=in_specs,
        out_specs=out_specs,
        out_shape=out_shapes,
        compiler_params=_params(1, est),
        name="sample_attn",
    )(qkv, *caches)


def _as_column(row_vec):
    n = row_vec.shape[1]
    eye = lax.broadcasted_iota(jnp.int32, (n, n), 0) == lax.broadcasted_iota(jnp.int32, (n, n), 1)
    return jnp.sum(jnp.where(eye, row_vec, 0.0), axis=1, keepdims=True)


def _sample_gla_kernel(q_ref, k_ref, v_ref, gk_ref, r_ref, gn_ref, st_ref, o_ref, new_ref):
    gn = gn_ref[...]
    for row in range(q_ref.shape[0]):
        for h in range(GLA_HEADS):
            dk = slice(h * GLA_HEAD_DK, (h + 1) * GLA_HEAD_DK)
            dv = slice(h * GLA_HEAD_DV, (h + 1) * GLA_HEAD_DV)
            decay = _as_column(jnp.exp(gk_ref[row, :, dk]))
            k_col = _as_column(k_ref[row, :, dk])
            q_col = _as_column(q_ref[row, :, dk])
            state = decay * st_ref[row, h] + k_col * v_ref[row, :, dv]
            new_ref[row, h] = state
            o = jnp.sum(q_col * state, axis=0, keepdims=True)
            rb = r_ref[row, :, dv]
            o_ref[row, :, dv] = _rms_normed(o, gn) * (rb * jax.nn.sigmoid(rb))


def _sample_gla(qb, kb, vb, gk, rb, gn, state):
    nb = qb.shape[0]
    rows = SAMPLE_GLA_ROWS

    def row_spec(width):
        return pl.BlockSpec((rows, 1, width), lambda i: (i, 0, 0))

    state_spec = pl.BlockSpec((rows,) + state.shape[1:], lambda i: (i, 0, 0, 0))
    est = 4 * rows * GLA_HEADS * GLA_HEAD_DK * GLA_HEAD_DV * 4 * 2
    return pl.pallas_call(
        _sample_gla_kernel,
        grid=(nb // rows,),
        in_specs=[row_spec(GLA_DK), row_spec(GLA_DK), row_spec(GLA_DV), row_spec(GLA_DK), row_spec(GLA_DV),
                  pl.BlockSpec((1, GLA_HEAD_DV), lambda i: (0, 0)), state_spec],
        out_specs=[row_spec(GLA_DV), state_spec],
        out_shape=[jax.ShapeDtypeStruct((nb, 1, GLA_DV), _F32), jax.ShapeDtypeStruct(state.shape, state.dtype)],
        compiler_params=_params(1, est),
        name="sample_gla",
    )(qb, kb, vb, gk, rb, gn, state)


def _merge_ffn_kernel(x_ref, oa_ref, ob_ref, ga_ref, gb_ref, wpa_ref, wpb_ref, wo_ref, g2_ref, wup_ref, wdn_ref,
                      gf_ref, y_ref):
    tm = x_ref.shape[0]
    sub = min(tm, MERGE_SUBTILE)
    for r0 in range(0, tm, sub):
        rows = slice(r0, r0 + sub)
        ya = jnp.dot(oa_ref[rows, :].astype(_BF16), wpa_ref[...], preferred_element_type=_F32)
        yb = jnp.dot(ob_ref[rows, :].astype(_BF16), wpb_ref[...], preferred_element_type=_F32)
        mix = (jax.nn.sigmoid(ga_ref[rows, :].astype(_F32)) * ya
               + jax.nn.sigmoid(gb_ref[rows, :].astype(_F32)) * yb)
        h = x_ref[rows, :] + jnp.dot(mix.astype(_BF16), wo_ref[...], preferred_element_type=_F32)
        hn = _rms_normed(h, g2_ref[...]).astype(_BF16)
        acc = h
        for c0 in range(0, D_FF, FF_CHUNK):
            f = jnp.maximum(jnp.dot(hn, wup_ref[:, c0:c0 + FF_CHUNK], preferred_element_type=_F32), 0.0)
            acc = acc + jnp.dot((f * f).astype(_BF16), wdn_ref[c0:c0 + FF_CHUNK, :], preferred_element_type=_F32)
        y_ref[rows, :] = _rms_normed(acc, gf_ref[...])


def _merge_ffn(x, oa, ob, ga, gb, wpa, wpb, wo, g2, wup, wdn, gf, tm):
    m, d = x.shape

    def rows(width):
        return pl.BlockSpec((tm, width), lambda i: (i, 0))

    weights = (wpa, wpb, wo, wup, wdn)
    est = (sum(w.size for w in weights) * 2 + 2 * tm * (2 * d * 4 + A_GROUP_WIDTH * 4 + 3 * d * 4)
           + MERGE_SUBTILE * (6 * d + 2 * FF_CHUNK) * 4)
    return pl.pallas_call(
        _merge_ffn_kernel,
        grid=(m // tm,),
        in_specs=[rows(d), rows(A_GROUP_WIDTH), rows(GLA_DV), rows(d), rows(d),
                  _resident(wpa.shape), _resident(wpb.shape), _resident(wo.shape), _resident((1, d)),
                  _resident(wup.shape), _resident(wdn.shape), _resident((1, d))],
        out_specs=rows(d),
        out_shape=jax.ShapeDtypeStruct((m, d), _F32),
        compiler_params=_params(1, est),
        name="merge_ffn",
    )(x, oa, ob, ga, gb, wpa, wpb, wo, g2, wup, wdn, gf)


def _prepare_w_in(w_in):
    w = w_in.astype(_BF16)
    return w, w[:, _GATES_OFFSET:_GATES_OFFSET + 2 * D_MODEL]


def kernel(x_prompt, x_sample, cache_a1_k, cache_a1_v, cache_a2_k, cache_a2_v, cache_a3_k, cache_a3_v,
           state_gla, g_norm1, w_in, w_gk2, b_gk, g_gla, w_pa, w_pb, w_o, g_norm2, w_up, w_down, g_final):
    assert g_norm1.shape[0] == 1, "one layer"
    b, t, d = x_prompt.shape
    nb = x_sample.shape[0]
    assert x_sample.shape[1] == 1, "one new token per sample row"

    w_all, w_gates = _prepare_w_in(w_in[0])
    w_gk = jnp.pad(w_gk2[0], ((0, LANES - GLA_GATE_RANK), (0, 0))).astype(_BF16)
    b_gk_row = b_gk[0][None, :]
    g1 = g_norm1[0][None, :]
    g2 = g_norm2[0][None, :]
    gf = g_final[None, :]
    gn = g_gla[0][None, :]
    wpa = w_pa[0].astype(_BF16)
    wpb = w_pb[0].astype(_BF16)
    wo = w_o[0].astype(_BF16)
    wup = w_up[0].astype(_BF16)
    wdn = w_down[0].astype(_BF16)

    cos_p, sin_p = _rope_tables(t, 0, 1)
    outs = _attn_proj_prompt(x_prompt, g1, w_all, cos_p, sin_p)
    up, packs, p_caches = outs[0], outs[1:4], outs[4:]
    xp = x_prompt.reshape(b * t, d)
    ga, gb, ob, p_gla = _gla_fused(up.reshape(b * t, d), w_all, w_gates, w_gk, b_gk_row, gn, t // GLA_ROW_TILE)
    oa = _dilated_attn(packs, t)
    y_prompt = _merge_ffn(xp, oa.reshape(b * t, A_GROUP_WIDTH), ob, ga, gb,
                          wpa, wpb, wo, g2, wup, wdn, gf, ROW_TILE).reshape(b, t, d)

    cos_s, sin_s = _rope_tables(nb, PAST_LEN, 0)
    qkv_s = _attn_proj_sample(x_sample.reshape(1, nb, d), g1, w_all, cos_s, sin_s).reshape(nb, 1, 3 * A_WIDTH)
    xs = x_sample.reshape(nb, d)
    qb_s, kb_s, vb_s, rb_s, ga_s, gb_s, gk_s = _gla_proj(xs, g1, w_all, w_gates, w_gk, b_gk_row)
    caches = [jnp.transpose(c[0], (0, 2, 3, 1)).reshape(nb, A_GROUP_WIDTH, c.shape[2])
              for c in (cache_a1_k, cache_a1_v, cache_a2_k, cache_a2_v, cache_a3_k, cache_a3_v)]
    s_outs = _sample_attn(qkv_s, caches)
    oa_s, s_caches = s_outs[0], s_outs[1:]

    def tok(z):
        return z.reshape(nb, 1, z.shape[-1])

    ob_s, s_gla = _sample_gla(tok(qb_s), tok(kb_s), tok(vb_s), tok(gk_s), tok(rb_s), gn, state_gla[0])
    y_sample = _merge_ffn(xs, oa_s.reshape(nb, A_GROUP_WIDTH), ob_s.reshape(nb, GLA_DV), ga_s, gb_s,
                          wpa, wpb, wo, g2, wup, wdn, gf, nb).reshape(nb, 1, d)

    def prompt_cache_out(z):
        return z.reshape(1, z.shape[0], z.shape[1], A_HEADS_PER_GROUP, A_HEAD_DIM)

    def sample_cache_out(z):
        z = z.reshape(z.shape[0], A_HEADS_PER_GROUP, A_HEAD_DIM, z.shape[2])
        return jnp.transpose(z, (0, 3, 1, 2))[None]

    return (y_prompt, y_sample, *[prompt_cache_out(z) for z in p_caches], p_gla[None],
            *[sample_cache_out(z) for z in s_caches], s_gla[None])
```
